```python
import jax, jax.numpy as jnp
from jax import lax
import numpy as np

D_MODEL = 1024
BATCH = 16
SEQ = 256
DEPTH = 4
DEC_BATCH = 8
DEC_SEQ = 2048
PAST_LEN = 512

GRID_W = 64
N_MIXERS = 2
N_SGU = (DEPTH + 1) // 2
N_RWKV = DEPTH // 2
CHUNK = 128
SGU_WIDTH = D_MODEL
SGU_GROUPS = 16
HEAD_DIM = 64
N_HEADS = D_MODEL // HEAD_DIM
LORA_W = 64
LORA_A = 64
LORA_V = 32
LORA_G = 128
N_DIR = 2
D_FF = 4 * D_MODEL
NORM_EPS = 1e-6
GN_EPS = 64e-5

kernel_name = "hybrid_sgu_rwkv7_diffusion_step"


def _rmsnorm(x, g):
    xf = x.astype(jnp.float32)
    y = xf * lax.rsqrt(jnp.mean(xf * xf, axis=-1, keepdims=True) + NORM_EPS)
    return y.astype(x.dtype) * g


def _layernorm(x, g, b, eps):
    xf = x.astype(jnp.float32)
    mu = jnp.mean(xf, axis=-1, keepdims=True)
    var = jnp.mean(jnp.square(xf - mu), axis=-1, keepdims=True)
    return ((xf - mu) * lax.rsqrt(var + eps)).astype(x.dtype) * g + b


def _centred_shift(x):
    xp = jnp.pad(x, ((0, 0), (1, 1), (0, 0)))
    return 0.5 * (xp[:, :-2] + xp[:, 2:]) - x


def _heads(t):
    return t.reshape(t.shape[:-1] + (N_HEADS, HEAD_DIM))


def _sgu_mixer(h, w_in, ln_g, ln_b, w_s, b_s, w_out):
    B, L, _ = h.shape
    n_chunks = L // CHUNK
    z = jax.nn.gelu(h @ w_in)
    u, v = jnp.split(z, 2, axis=-1)
    v = _layernorm(v, ln_g, ln_b, NORM_EPS)
    v = v.reshape(B, n_chunks, CHUNK, SGU_GROUPS, SGU_WIDTH // SGU_GROUPS)
    v = jnp.einsum('gts,bcsgd->bctgd', w_s, v) + b_s.T[:, :, None]
    return (u * v.reshape(B, L, SGU_WIDTH)) @ w_out


def _wkv_scan(s0, r, decay, k, v, kk, a, reverse):
    def step(s, inp):
        r_t, w_t, k_t, v_t, kk_t, a_t = inp
        sa = jnp.einsum('bhij,bhj->bhi', s, -kk_t)
        s = (s * w_t[:, :, None, :]
             + sa[..., None] * (kk_t * a_t)[:, :, None, :]
             + v_t[..., None] * k_t[:, :, None, :])
        return s, jnp.einsum('bhij,bhj->bhi', s, r_t)
    xs = tuple(jnp.swapaxes(t, 0, 1) for t in (r, decay, k, v, kk, a))
    s_final, ys = lax.scan(step, s0, xs, reverse=reverse)
    return s_final, jnp.swapaxes(ys, 0, 1)


def _rwkv_mixer(h, s0, v_first, P, i):
    B, L, D = h.shape
    f32 = jnp.float32
    xx = _centred_shift(h)
    mu = P['rwkv_mu'][i]
    xr, xw, xk, xv, xa, xg = [h + xx * mu[j] for j in range(6)]
    r = _heads(xr @ P['rwkv_w_r'][i])
    k = _heads(xk @ P['rwkv_w_k'][i])
    v = xv @ P['rwkv_w_v'][i]
    if v_first is None:
        v_first = v
    else:
        j = i - 1
        v = v + (v_first - v) * jax.nn.sigmoid(P['rwkv_v0'][j] + (xv @ P['rwkv_v1'][j]) @ P['rwkv_v2'][j])
    v = _heads(v)
    g = jax.nn.sigmoid(xg @ P['rwkv_g1'][i]) @ P['rwkv_g2'][i]
    kk = (k * _heads(P['rwkv_k_k'][i])).astype(f32)
    kk = kk / jnp.maximum(jnp.sqrt(jnp.sum(kk * kk, axis=-1, keepdims=True)), 1e-12)
    k_a = _heads(P['rwkv_k_a'][i])
    r_k = P['rwkv_r_k'][i]
    y = jnp.zeros(r.shape, f32)
    bonus = 0.0
    finals = []
    for d in range(N_DIR):
        z = (P['rwkv_w0'][i, d] + jnp.tanh(xw @ P['rwkv_w1'][i, d]) @ P['rwkv_w2'][i, d]).astype(f32)
        decay = _heads(jnp.exp(-jnp.exp(-jax.nn.softplus(-z) - 0.5)))
        a = _heads(jax.nn.sigmoid(P['rwkv_a0'][i, d] + (xa @ P['rwkv_a1'][i, d]) @ P['rwkv_a2'][i, d]))
        k_d = k * (1 + (a - 1) * k_a)
        s_fin, y_d = _wkv_scan(s0[:, d].astype(f32), r.astype(f32), decay, k_d.astype(f32),
                               v.astype(f32), kk, a.astype(f32), reverse=(d == 1))
        y = y + y_d
        bonus = bonus + jnp.sum(r * k_d * r_k, axis=-1, keepdims=True)
        finals.append(s_fin)
    y = _layernorm(y, _heads(P['rwkv_ln_g'][i]), _heads(P['rwkv_ln_b'][i]), GN_EPS).astype(h.dtype)
    y = y + bonus * v
    out = (y.reshape(B, L, D) * g) @ P['rwkv_w_o'][i]
    return out, jnp.stack(finals, axis=1), v_first


def _trunk(x, cond, s_init, P):
    B = x.shape[0]
    ada_in = jax.nn.silu(cond)
    v_first = None
    finals = []
    for l in range(DEPTH):
        mod = (ada_in @ P['ada_w'][l] + P['ada_b'][l])[:, None, :]
        sh1, sc1, g1, sh2, sc2, g2 = jnp.split(mod, 6, axis=-1)
        h = _rmsnorm(x, P['norm1_g'][l]) * (1 + sc1) + sh1
        i = l // N_MIXERS
        if l % N_MIXERS == 0:
            m = _sgu_mixer(h, P['sgu_w_in'][i], P['sgu_ln_g'][i], P['sgu_ln_b'][i],
                           P['sgu_w_s'][i], P['sgu_b_s'][i], P['sgu_w_out'][i])
        else:
            if s_init is None:
                s0 = jnp.zeros((B, N_DIR, N_HEADS, HEAD_DIM, HEAD_DIM), jnp.float32)
            else:
                s0 = s_init[:, i]
            m, s_fin, v_first = _rwkv_mixer(h, s0, v_first, P, i)
            finals.append(s_fin)
        x = x + g1 * m
        h = _rmsnorm(x, P['norm2_g'][l]) * (1 + sc2) + sh2
        x = x + g2 * (jnp.square(jax.nn.relu(h @ P['mlp_w1'][l])) @ P['mlp_w2'][l])
    return _rmsnorm(x, P['final_g']), jnp.stack(finals, axis=1)


def setup_inputs(seed: int = 0) -> dict:
    key = jax.random.key(seed)
    ks = iter(jax.random.split(key, 48))

    def nrm(shape, scale):
        return jax.random.normal(next(ks), shape, jnp.float32) * scale

    def unif(shape, lo, hi):
        return jax.random.uniform(next(ks), shape, jnp.float32, lo, hi)

    D, H, N = D_MODEL, N_HEADS, HEAD_DIM
    NV = max(N_RWKV - 1, 0)
    return {
        'x_prompt': nrm((BATCH, SEQ, D), 1.0),
        'x_sample': nrm((DEC_BATCH, DEC_SEQ, D), 1.0),
        'state_wkv': nrm((DEC_BATCH, N_RWKV, N_DIR, H, N, N), 1.0),
        'c': nrm((DEC_BATCH, D), 1.0),
        'c_ctx': nrm((D,), 1.0),
        'norm1_g': 1.0 + nrm((DEPTH, D), 0.02),
        'norm2_g': 1.0 + nrm((DEPTH, D), 0.02),
        'ada_w': nrm((DEPTH, D, 6 * D), 0.5 * D ** -0.5),
        'ada_b': nrm((DEPTH, 6 * D), 0.02),
        'sgu_w_in': nrm((N_SGU, D, 2 * SGU_WIDTH), D ** -0.5),
        'sgu_ln_g': 1.0 + nrm((N_SGU, SGU_WIDTH), 0.02),
        'sgu_ln_b': nrm((N_SGU, SGU_WIDTH), 0.02),
        'sgu_w_s': nrm((N_SGU, SGU_GROUPS, CHUNK, CHUNK), CHUNK ** -0.5),
        'sgu_b_s': 1.0 + nrm((N_SGU, SGU_GROUPS, CHUNK), 0.02),
        'sgu_w_out': nrm((N_SGU, SGU_WIDTH, D), SGU_WIDTH ** -0.5),
        'rwkv_mu': unif((N_RWKV, 6, D), 0.0, 1.0),
        'rwkv_w_r': nrm((N_RWKV, D, D), D ** -0.5),
        'rwkv_w_k': nrm((N_RWKV, D, D), D ** -0.5),
        'rwkv_w_v': nrm((N_RWKV, D, D), D ** -0.5),
        'rwkv_w_o': nrm((N_RWKV, D, D), D ** -0.5),
        'rwkv_w0': unif((N_RWKV, N_DIR, D), -6.0, 1.0),
        'rwkv_w1': nrm((N_RWKV, N_DIR, D, LORA_W), D ** -0.5),
        'rwkv_w2': nrm((N_RWKV, N_DIR, LORA_W, D), 0.1 * LORA_W ** -0.5),
        'rwkv_a0': nrm((N_RWKV, N_DIR, D), 0.1),
        'rwkv_a1': nrm((N_RWKV, N_DIR, D, LORA_A), D ** -0.5),
        'rwkv_a2': nrm((N_RWKV, N_DIR, LORA_A, D), 0.5 * LORA_A ** -0.5),
        'rwkv_v0': nrm((NV, D), 0.1),
        'rwkv_v1': nrm((NV, D, LORA_V), D ** -0.5),
        'rwkv_v2': nrm((NV, LORA_V, D), 0.5 * LORA_V ** -0.5),
        'rwkv_g1': nrm((N_RWKV, D, LORA_G), D ** -0.5),
        'rwkv_g2': nrm((N_RWKV, LORA_G, D), LORA_G ** -0.5),
        'rwkv_k_k': 0.85 + nrm((N_RWKV, D), 0.02),
        'rwkv_k_a': 1.0 + nrm((N_RWKV, D), 0.02),
        'rwkv_r_k': nrm((N_RWKV, H, N), 0.1),
        'rwkv_ln_g': 1.0 + nrm((N_RWKV, D), 0.02),
        'rwkv_ln_b': nrm((N_RWKV, D), 0.02),
        'mlp_w1': nrm((DEPTH, D, D_FF), D ** -0.5),
        'mlp_w2': nrm((DEPTH, D_FF, D), D_FF ** -0.5),
        'final_g': 1.0 + nrm((D,), 0.02),
    }


def reference(x_prompt, x_sample, state_wkv, c, c_ctx, norm1_g, norm2_g, ada_w, ada_b,
              sgu_w_in, sgu_ln_g, sgu_ln_b, sgu_w_s, sgu_b_s, sgu_w_out,
              rwkv_mu, rwkv_w_r, rwkv_w_k, rwkv_w_v, rwkv_w_o, rwkv_w0, rwkv_w1, rwkv_w2,
              rwkv_a0, rwkv_a1, rwkv_a2, rwkv_v0, rwkv_v1, rwkv_v2, rwkv_g1, rwkv_g2,
              rwkv_k_k, rwkv_k_a, rwkv_r_k, rwkv_ln_g, rwkv_ln_b, mlp_w1, mlp_w2, final_g):
    P = dict(norm1_g=norm1_g, norm2_g=norm2_g, ada_w=ada_w, ada_b=ada_b,
             sgu_w_in=sgu_w_in, sgu_ln_g=sgu_ln_g, sgu_ln_b=sgu_ln_b, sgu_w_s=sgu_w_s,
             sgu_b_s=sgu_b_s, sgu_w_out=sgu_w_out,
             rwkv_mu=rwkv_mu, rwkv_w_r=rwkv_w_r, rwkv_w_k=rwkv_w_k, rwkv_w_v=rwkv_w_v,
             rwkv_w_o=rwkv_w_o, rwkv_w0=rwkv_w0, rwkv_w1=rwkv_w1, rwkv_w2=rwkv_w2,
             rwkv_a0=rwkv_a0, rwkv_a1=rwkv_a1, rwkv_a2=rwkv_a2, rwkv_v0=rwkv_v0,
             rwkv_v1=rwkv_v1, rwkv_v2=rwkv_v2, rwkv_g1=rwkv_g1, rwkv_g2=rwkv_g2,
             rwkv_k_k=rwkv_k_k, rwkv_k_a=rwkv_k_a, rwkv_r_k=rwkv_r_k,
             rwkv_ln_g=rwkv_ln_g, rwkv_ln_b=rwkv_ln_b,
             mlp_w1=mlp_w1, mlp_w2=mlp_w2, final_g=final_g)
    y_prompt, new_state_wkv = _trunk(x_prompt, c_ctx[None, :], None, P)
    y_sample, _ = _trunk(x_sample, c, state_wkv, P)
    return (y_prompt, y_sample, new_state_wkv)
```

```python
import functools

import jax
import jax.numpy as jnp
from jax import lax
from jax.experimental import pallas as pl
from jax.experimental.pallas import tpu as pltpu

D_MODEL = 1024
BATCH = 16
SEQ = 256
DEPTH = 4
DEC_BATCH = 8
DEC_SEQ = 2048
N_RWKV = DEPTH // 2
CHUNK = 128
SGU_GROUPS = 16
HEAD_DIM = 64
N_HEADS = D_MODEL // HEAD_DIM
LORA_V = 32
D_FF = 4 * D_MODEL
NORM_EPS = 1e-6
GN_EPS = 64e-5

LANES = 128
N_PAIRS = D_MODEL // LANES
N_CTX = BATCH * SEQ
N_LAT = DEC_BATCH * DEC_SEQ
N_TOK = N_CTX + N_LAT
N_SEQ = BATCH + DEC_BATCH
COND_ROWS = 16

WKV_T = 64
NC_CTX = SEQ // WKV_T
NC_LAT = DEC_SEQ // WKV_T
WKV_ITEMS_CTX = BATCH * 2 * NC_CTX
WKV_ITEMS = WKV_ITEMS_CTX + DEC_BATCH * 2 * NC_LAT

TM_SGU = 256
TM_MLP = 512
TM_PROJ = 128
TM_POST = 256
TM_NORM = 512
FF_CHUNK = 1024
ADA_TN = 1536
VMEM_LIMIT = 56 * 1024 * 1024

F32 = jnp.float32
BF16 = jnp.bfloat16


def _dot(a, b):
    return jnp.dot(a, b, preferred_element_type=F32)


def _dot_nt(a, b):
    return lax.dot_general(a, b, (((1,), (1,)), ((), ())), preferred_element_type=F32)


def _dot_tn(a, b):
    return lax.dot_general(a, b, (((0,), (0,)), ((), ())), preferred_element_type=F32)


def _split3(x):
    h1 = x.astype(BF16)
    r1 = x - h1.astype(F32)
    h2 = r1.astype(BF16)
    h3 = (r1 - h2.astype(F32)).astype(BF16)
    return h1, h2, h3


def _dot_exact_rhs(x, m):
    h1, h2, h3 = _split3(x)
    return _dot(h1, m) + _dot(h2, m) + _dot(h3, m)


def _dot_exact_lhs(m, x):
    h1, h2, h3 = _split3(x)
    return _dot(m, h1) + _dot(m, h2) + _dot(m, h3)


def _sigmoid(x):
    return 1.0 / (1.0 + jnp.exp(-x))


def _norm_mod(x, g, scale, shift):
    ms = jnp.mean(x * x, axis=-1, keepdims=True)
    return (x * lax.rsqrt(ms + NORM_EPS) * g) * (1.0 + scale) + shift


def _stack_heads(x, lo):
    return jnp.concatenate([jnp.where(lo, x, 0.0), jnp.where(lo, 0.0, x)], axis=0)


def _cond_row(i, tm):
    n_ctx_blocks = N_CTX // tm
    return jnp.where(i < n_ctx_blocks, 0, 1 + (i - n_ctx_blocks) // (DEC_SEQ // tm))


def _const_spec(shape):
    nd = len(shape)
    return pl.BlockSpec(shape, lambda *_: (0,) * nd, pipeline_mode=pl.Buffered(1))


def _mod_spec(tm):
    return pl.BlockSpec((None, 1, 6 * D_MODEL), lambda i: (_cond_row(i, tm), 0, 0))


def _params(n_axes=1):
    return pltpu.CompilerParams(
        dimension_semantics=("arbitrary",) * n_axes, vmem_limit_bytes=VMEM_LIMIT)


def _ada_kernel(cond_ref, w_ref, b_ref, o_ref):
    c = cond_ref[...]
    s = c * _sigmoid(c)
    o_ref[...] = _dot(s.astype(BF16), w_ref[...].astype(BF16)) + b_ref[...]


def _ada_call(cond, ada_w, ada_b):
    return pl.pallas_call(
        _ada_kernel,
        grid=(DEPTH, 6 * D_MODEL // ADA_TN),
        in_specs=[
            pl.BlockSpec((COND_ROWS, D_MODEL), lambda l, n: (0, 0)),
            pl.BlockSpec((None, D_MODEL, ADA_TN), lambda l, n: (l, 0, n)),
            pl.BlockSpec((None, 1, ADA_TN), lambda l, n: (l, 0, n)),
        ],
        out_specs=pl.BlockSpec((None, COND_ROWS, ADA_TN), lambda l, n: (l, 0, n)),
        out_shape=jax.ShapeDtypeStruct((DEPTH, COND_ROWS, 6 * D_MODEL), F32),
        compiler_params=_params(2),
        name="ada_mod",
    )(cond, ada_w, ada_b.reshape(DEPTH, 1, 6 * D_MODEL))


def _sgu_kernel(x_ref, mod_ref, ng_ref, win_ref, lng_ref, lnb_ref, wp_ref, bmap_ref, wout_ref, o_ref):
    x = x_ref[...]
    mod = mod_ref[...]
    shift, scale, gate = (mod[:, j * D_MODEL:(j + 1) * D_MODEL] for j in range(3))
    h = _norm_mod(x, ng_ref[...], scale, shift)
    z = _dot(h.astype(BF16), win_ref[...])
    z = z * (0.5 * (1.0 + jnp.tanh(0.7978845608028654 * (z + 0.044715 * (z * z * z)))))
    u = z[:, :D_MODEL]
    v = z[:, D_MODEL:]
    mu = jnp.mean(v, axis=-1, keepdims=True)
    dv = v - mu
    var = jnp.mean(dv * dv, axis=-1, keepdims=True)
    v = dv * lax.rsqrt(var + NORM_EPS) * lng_ref[...] + lnb_ref[...]
    lo = lax.broadcasted_iota(jnp.int32, (CHUNK, LANES), 1) < HEAD_DIM
    bmap = bmap_ref[...]
    rows = []
    for c in range(TM_SGU // CHUNK):
        vc = v[c * CHUNK:(c + 1) * CHUNK]
        cols = []
        for p in range(N_PAIRS):
            rhs = _stack_heads(vc[:, p * LANES:(p + 1) * LANES], lo).astype(BF16)
            cols.append(_dot(wp_ref[p], rhs))
        rows.append(jnp.concatenate(cols, axis=1) + bmap)
    vs = jnp.concatenate(rows, axis=0)
    m = _dot((u * vs).astype(BF16), wout_ref[...])
    o_ref[...] = x + gate * m


def _sgu_call(x, mod, ng, w_in, ln_g, ln_b, w_pair, bmap, w_out):
    row = pl.BlockSpec((TM_SGU, D_MODEL), lambda i: (i, 0))
    return pl.pallas_call(
        _sgu_kernel,
        grid=(N_TOK // TM_SGU,),
        in_specs=[
            row, _mod_spec(TM_SGU),
            _const_spec((1, D_MODEL)),
            _const_spec((D_MODEL, 2 * D_MODEL)),
            _const_spec((1, D_MODEL)), _const_spec((1, D_MODEL)),
            _const_spec((N_PAIRS, CHUNK, 2 * CHUNK)),
            _const_spec((CHUNK, D_MODEL)),
            _const_spec((D_MODEL, D_MODEL)),
        ],
        out_specs=row,
        out_shape=jax.ShapeDtypeStruct((N_TOK, D_MODEL), F32),
        compiler_params=_params(),
        name="sgu_layer",
    )(x, mod, ng, w_in, ln_g, ln_b, w_pair, bmap, w_out)


def _mlp_kernel(x_ref, mod_ref, ng_ref, w1_ref, w2_ref, o_ref):
    x = x_ref[...]
    mod = mod_ref[...]
    shift, scale, gate = (mod[:, j * D_MODEL:(j + 1) * D_MODEL] for j in range(3, 6))
    h = _norm_mod(x, ng_ref[...], scale, shift).astype(BF16)
    acc = jnp.zeros((TM_MLP, D_MODEL), F32)
    for c in range(D_FF // FF_CHUNK):
        a = _dot(h, w1_ref[:, c * FF_CHUNK:(c + 1) * FF_CHUNK])
        a = jnp.square(jnp.maximum(a, 0.0))
        acc = acc + _dot(a.astype(BF16), w2_ref[c * FF_CHUNK:(c + 1) * FF_CHUNK, :])
    o_ref[...] = x + gate * acc


def _mlp_call(x, mod, ng, w1, w2):
    row = pl.BlockSpec((TM_MLP, D_MODEL), lambda i: (i, 0))
    return pl.pallas_call(
        _mlp_kernel,
        grid=(N_TOK // TM_MLP,),
        in_specs=[row, _mod_spec(TM_MLP), _const_spec((1, D_MODEL)),
                  _const_spec((D_MODEL, D_FF)), _const_spec((D_FF, D_MODEL))],
        out_specs=row,
        out_shape=jax.ShapeDtypeStruct((N_TOK, D_MODEL), F32),
        compiler_params=_params(),
        name="mlp_layer",
    )(x, mod, ng, w1, w2)


def _proj_kernel(has_vmix, *refs):
    if has_vmix:
        (x_ref, xp_ref, xn_ref, mod_ref, ng_ref, mu_ref, wr_ref, wk_ref, wv_ref, g1_ref, g2_ref,
         w0_ref, w1_ref, w2_ref, a0_ref, a1_ref, a2_ref, ka_ref, rk_ref, e_ref, et_ref,
         vf_ref, v0_ref, v1_ref, v2_ref,
         r_out, k_out, v_out, g_out, bv_out, lw_out, a_out) = refs
    else:
        (x_ref, xp_ref, xn_ref, mod_ref, ng_ref, mu_ref, wr_ref, wk_ref, wv_ref, g1_ref, g2_ref,
         w0_ref, w1_ref, w2_ref, a0_ref, a1_ref, a2_ref, ka_ref, rk_ref, e_ref, et_ref,
         r_out, k_out, v_out, g_out, bv_out, lw_out, a_out) = refs
    tm = TM_PROJ
    i = pl.program_id(0)
    t0 = i * tm
    in_ctx = t0 < N_CTX
    pos = jnp.where(in_ctx, t0 & (SEQ - 1), (t0 - N_CTX) & (DEC_SEQ - 1))
    seq_len = jnp.where(in_ctx, SEQ, DEC_SEQ)
    has_prev = jnp.where(pos != 0, 1.0, 0.0)
    has_next = jnp.where(pos + tm != seq_len, 1.0, 0.0)

    mod = mod_ref[...]
    shift, scale = mod[:, :D_MODEL], mod[:, D_MODEL:2 * D_MODEL]
    ng = ng_ref[...]
    h = _norm_mod(x_ref[...], ng, scale, shift)
    h_prev = _norm_mod(xp_ref[7:8, :], ng, scale, shift) * has_prev
    h_next = _norm_mod(xn_ref[0:1, :], ng, scale, shift) * has_next
    rows = lax.broadcasted_iota(jnp.int32, (tm, D_MODEL), 0)
    h_dn = jnp.where(rows == 0, h_prev, pltpu.roll(h, 1, 0))
    h_up = jnp.where(rows == tm - 1, h_next, pltpu.roll(h, tm - 1, 0))
    xx = 0.5 * (h_dn + h_up) - h
    mu = mu_ref[...]
    xr, xw, xk, xv, xa, xg = ((h + xx * mu[j:j + 1, :]).astype(BF16) for j in range(6))

    r = _dot(xr, wr_ref[...])
    k = _dot(xk, wk_ref[...])
    v = _dot(xv, wv_ref[...])
    if has_vmix:
        mix = _sigmoid(v0_ref[...] + _dot(_dot(xv, v1_ref[...]).astype(BF16), v2_ref[...]))
        v = v + (vf_ref[...] - v) * mix
    g = _dot(_sigmoid(_dot(xg, g1_ref[...])).astype(BF16), g2_ref[...])

    lo = lax.broadcasted_iota(jnp.int32, (tm, LANES), 1) < HEAD_DIM
    th = jnp.tanh(_dot(xw, w1_ref[...]))
    al = _dot(xa, a1_ref[...])
    ka = ka_ref[...]
    ksum = jnp.zeros_like(k)
    for d in range(2):
        keep = lo if d == 0 else jnp.logical_not(lo)
        z = w0_ref[d:d + 1, :] + _dot(jnp.where(keep, th, 0.0).astype(BF16), w2_ref[...])
        lw_out[d] = -0.6065306597126334 * _sigmoid(z)
        a_d = _sigmoid(a0_ref[d:d + 1, :] + _dot(jnp.where(keep, al, 0.0).astype(BF16), a2_ref[...]))
        a_out[d] = a_d
        ksum = ksum + k * (1.0 + (a_d - 1.0) * ka)
    head_sum = _dot_exact_rhs(r * ksum * rk_ref[...], e_ref[...])
    bonus = _dot_exact_rhs(head_sum, et_ref[...])
    r_out[...] = r
    k_out[...] = k
    v_out[...] = v
    g_out[...] = g
    bv_out[...] = bonus * v


def _proj_call(x, mod, ng, W, i, v_first):
    tm = TM_PROJ
    nb8 = tm // 8
    has_vmix = v_first is not None
    row = pl.BlockSpec((tm, D_MODEL), lambda i: (i, 0))
    row2 = pl.BlockSpec((2, tm, D_MODEL), lambda i: (0, i, 0))
    prev = pl.BlockSpec((8, D_MODEL), lambda i: (jnp.maximum(i * nb8 - 1, 0), 0))
    nxt = pl.BlockSpec((8, D_MODEL), lambda i: (jnp.minimum((i + 1) * nb8, N_TOK // 8 - 1), 0))
    mat = _const_spec((D_MODEL, D_MODEL))
    vec = _const_spec((1, D_MODEL))
    vec2 = _const_spec((2, D_MODEL))
    down = _const_spec((D_MODEL, LANES))
    up = _const_spec((LANES, D_MODEL))
    in_specs = [row, prev, nxt, _mod_spec(tm), vec, _const_spec((6, D_MODEL)),
                mat, mat, mat, down, up,
                vec2, down, up, vec2, down, up, vec, vec, down, up]
    args = [x, x, x, mod, ng, W['mu'][i], W['w_r'][i], W['w_k'][i], W['w_v'][i], W['g1'][i], W['g2'][i],
            W['w0'][i], W['w1'][i], W['w2'][i], W['a0'][i], W['a1'][i], W['a2'][i],
            W['k_a'][i], W['r_k'][i], W['e'], W['et']]
    if has_vmix:
        in_specs += [row, vec, down, up]
        args += [v_first, W['v0'][i - 1], W['v1'][i - 1], W['v2'][i - 1]]
    tok = jax.ShapeDtypeStruct((N_TOK, D_MODEL), F32)
    tok2 = jax.ShapeDtypeStruct((2, N_TOK, D_MODEL), F32)
    return pl.pallas_call(
        functools.partial(_proj_kernel, has_vmix),
        grid=(N_TOK // tm,),
        in_specs=in_specs,
        out_specs=[row, row, row, row, row, row2, row2],
        out_shape=[tok, tok, tok, tok, tok, tok2, tok2],
        compiler_params=_params(),
        name="rwkv_proj",
    )(*args)


def _wkv_decode(i):
    in_ctx = i < WKV_ITEMS_CTX
    il = jnp.maximum(i - WKV_ITEMS_CTX, 0)
    seq = jnp.where(in_ctx, i // (2 * NC_CTX), BATCH + il // (2 * NC_LAT))
    d = jnp.where(in_ctx, (i // NC_CTX) % 2, (il // NC_LAT) % 2)
    j = jnp.where(in_ctx, i % NC_CTX, il % NC_LAT)
    nc = jnp.where(in_ctx, NC_CTX, NC_LAT)
    cidx = jnp.where(d == 0, j, nc - 1 - j)
    base = jnp.where(in_ctx, (i // (2 * NC_CTX)) * NC_CTX, N_CTX // WKV_T + (il // (2 * NC_LAT)) * NC_LAT)
    return seq, d, j, nc, base + cidx


def _wkv_kernel(r_ref, k_ref, v_ref, lw_ref, a_ref, kkw_ref, ka_ref, s0_ref, y_ref, sf_ref, s_scr):
    T = WKV_T
    _, d, j, nc, _ = _wkv_decode(pl.program_id(0))
    lo = lax.broadcasted_iota(jnp.int32, (T, LANES), 1) < HEAD_DIM

    @pl.when(j == 0)
    def _():
        for p in range(N_PAIRS):
            s_scr[p] = _stack_heads(s0_ref[:, p * LANES:(p + 1) * LANES], lo)

    sgn = 1 - 2 * d
    tt = lax.broadcasted_iota(jnp.int32, (T, T), 0)
    ss = lax.broadcasted_iota(jnp.int32, (T, T), 1)
    cum = jnp.where(sgn * (tt - ss) + 1 > 0, 1.0, 0.0).astype(BF16)
    rho = lax.broadcasted_iota(jnp.int32, (4 * T, 4 * T), 0)
    gam = lax.broadcasted_iota(jnp.int32, (4 * T, 4 * T), 1)
    gmask = sgn * ((rho & (T - 1)) - (gam & (T - 1))) + (rho >> 7) > 0
    lvl = (lax.broadcasted_iota(jnp.int32, (2 * T, 2 * T), 0)
           ^ lax.broadcasted_iota(jnp.int32, (2 * T, 2 * T), 1))

    lw = lw_ref[...]
    logc = _dot_exact_lhs(cum, lw)
    tot = jnp.sum(lw, axis=0, keepdims=True)
    c = jnp.exp(logc)
    c_inv = jnp.exp(-logc)
    c_prev = jnp.exp(logc - lw)
    c_end = jnp.exp(tot - logc)
    c_tot = jnp.exp(tot)

    r = r_ref[...]
    k = k_ref[...]
    v = v_ref[...]
    a = a_ref[...]
    kk_raw = k * kkw_ref[...]
    kd = k * (1.0 + (a - 1.0) * ka_ref[...])

    for p in range(N_PAIRS):
        sl = slice(p * LANES, (p + 1) * LANES)
        kq = kk_raw[:, sl]
        sq = kq * kq
        n_lo = jnp.sqrt(jnp.sum(jnp.where(lo, sq, 0.0), axis=-1, keepdims=True))
        n_hi = jnp.sqrt(jnp.sum(jnp.where(lo, 0.0, sq), axis=-1, keepdims=True))
        kk = kq / jnp.maximum(jnp.where(lo, n_lo, n_hi), 1e-12)
        beta = kk * a[:, sl]
        a_t = -kk * c_prev[:, sl]
        r_t = r[:, sl] * c[:, sl]
        b_t = beta * c_inv[:, sl]
        k_t = kd[:, sl] * c_inv[:, sl]
        ar = jnp.concatenate([_stack_heads(a_t, lo), _stack_heads(r_t, lo)], axis=0).astype(BF16)
        bk = jnp.concatenate([_stack_heads(b_t, lo), _stack_heads(k_t, lo)], axis=0).astype(BF16)
        gm = jnp.where(gmask, _dot_nt(ar, bk), 0.0)
        s_old = s_scr[p]
        uy0 = _dot_nt(ar, s_old.astype(BF16))
        v2 = _stack_heads(v[:, sl], lo)
        x = uy0[:2 * T] + _dot(gm[:2 * T, 2 * T:].astype(BF16), v2.astype(BF16))
        pw = gm[:2 * T, :2 * T]
        w = jnp.where(lvl == 1, pw, 0.0)
        for lb in range(1, 6):
            p_off = jnp.where((lvl >> lb) == 1, pw, 0.0)
            t1 = p_off + _dot(p_off.astype(BF16), w.astype(BF16))
            w = w + t1 + _dot(w.astype(BF16), t1.astype(BF16))
        x = x + _dot(w.astype(BF16), x.astype(BF16))
        uv = jnp.concatenate([x, v2], axis=0).astype(BF16)
        y2 = uy0[2 * T:] + _dot(gm[2 * T:].astype(BF16), uv)
        y_ref[:, sl] = y2[:T] + y2[T:]
        bkc = jnp.concatenate([_stack_heads(beta * c_end[:, sl], lo),
                               _stack_heads(kd[:, sl] * c_end[:, sl], lo)], axis=0).astype(BF16)
        s_new = s_old * c_tot[:, sl] + _dot_tn(uv, bkc)
        s_scr[p] = s_new

        @pl.when(j == nc - 1)
        def _():
            sf_ref[:, sl] = s_new[:HEAD_DIM] + s_new[HEAD_DIM:]


def _wkv_call(r, k, v, lw, a, kkw, ka, s0):
    tok = pl.BlockSpec((WKV_T, D_MODEL), lambda i: (_wkv_decode(i)[4], 0))
    tok_d = pl.BlockSpec((None, WKV_T, D_MODEL), lambda i: (_wkv_decode(i)[1], _wkv_decode(i)[4], 0))
    state = pl.BlockSpec((None, None, HEAD_DIM, D_MODEL),
                         lambda i: (_wkv_decode(i)[0], _wkv_decode(i)[1], 0, 0))
    vec = _const_spec((1, D_MODEL))
    return pl.pallas_call(
        _wkv_kernel,
        grid=(WKV_ITEMS,),
        in_specs=[tok, tok, tok, tok_d, tok_d, vec, vec, state],
        out_specs=[tok_d, state],
        out_shape=[jax.ShapeDtypeStruct((2, N_TOK, D_MODEL), F32),
                   jax.ShapeDtypeStruct((N_SEQ, 2, HEAD_DIM, D_MODEL), F32)],
        scratch_shapes=[pltpu.VMEM((N_PAIRS, 2 * HEAD_DIM, LANES), F32)],
        compiler_params=_params(),
        name="wkv_scan",
    )(r, k, v, lw, a, kkw, ka, s0)


def _post_kernel(x_ref, mod_ref, y_ref, bv_ref, g_ref, lng_ref, lnb_ref, e_ref, et_ref, wo_ref, o_ref):
    gate = mod_ref[...][:, 2 * D_MODEL:3 * D_MODEL]
    y = y_ref[0] + y_ref[1]
    e = e_ref[...]
    et = et_ref[...]
    inv_n = 1.0 / HEAD_DIM
    mu = _dot_exact_rhs(_dot_exact_rhs(y, e), et) * inv_n
    dv = y - mu
    var = _dot_exact_rhs(_dot_exact_rhs(dv * dv, e), et) * inv_n
    yn = dv * lax.rsqrt(var + GN_EPS) * lng_ref[...] + lnb_ref[...]
    yy = (yn + bv_ref[...]) * g_ref[...]
    o_ref[...] = x_ref[...] + gate * _dot(yy.astype(BF16), wo_ref[...])


def _post_call(x, mod, y, bv, g, ln_g, ln_b, e, et, w_o):
    tm = TM_POST
    row = pl.BlockSpec((tm, D_MODEL), lambda i: (i, 0))
    row2 = pl.BlockSpec((2, tm, D_MODEL), lambda i: (0, i, 0))
    vec = _const_spec((1, D_MODEL))
    return pl.pallas_call(
        _post_kernel,
        grid=(N_TOK // tm,),
        in_specs=[row, _mod_spec(tm), row2, row, row, vec, vec,
                  _const_spec((D_MODEL, LANES)), _const_spec((LANES, D_MODEL)),
                  _const_spec((D_MODEL, D_MODEL))],
        out_specs=row,
        out_shape=jax.ShapeDtypeStruct((N_TOK, D_MODEL), F32),
        compiler_params=_params(),
        name="rwkv_out",
    )(x, mod, y, bv, g, ln_g, ln_b, e, et, w_o)


def _final_kernel(x_ref, g_ref, o_ref):
    x = x_ref[...]
    ms = jnp.mean(x * x, axis=-1, keepdims=True)
    o_ref[...] = x * lax.rsqrt(ms + NORM_EPS) * g_ref[...]


def _final_call(x, g, row0, n_rows):
    tm = TM_NORM
    b0 = row0 // tm
    return pl.pallas_call(
        _final_kernel,
        grid=(n_rows // tm,),
        in_specs=[pl.BlockSpec((tm, D_MODEL), lambda i: (i + b0, 0)), _const_spec((1, D_MODEL))],
        out_specs=pl.BlockSpec((tm, D_MODEL), lambda i: (i, 0)),
        out_shape=jax.ShapeDtypeStruct((n_rows, D_MODEL), F32),
        compiler_params=_params(),
        name="final_norm",
    )(x, g)


def _pad_cols(w, n):
    return jnp.pad(w, [(0, 0)] * (w.ndim - 1) + [(0, n - w.shape[-1])])


def _pad_rows(w, n):
    return jnp.pad(w, [(0, 0)] * (w.ndim - 2) + [(0, n - w.shape[-2]), (0, 0)])


def kernel(x_prompt, x_sample, state_wkv, c, c_ctx, norm1_g, norm2_g, ada_w, ada_b, sgu_w_in, sgu_ln_g,
           sgu_ln_b, sgu_w_s, sgu_b_s, sgu_w_out, rwkv_mu, rwkv_w_r, rwkv_w_k, rwkv_w_v, rwkv_w_o,
           rwkv_w0, rwkv_w1, rwkv_w2, rwkv_a0, rwkv_a1, rwkv_a2, rwkv_v0, rwkv_v1, rwkv_v2, rwkv_g1,
           rwkv_g2, rwkv_k_k, rwkv_k_a, rwkv_r_k, rwkv_ln_g, rwkv_ln_b, mlp_w1, mlp_w2, final_g):
    D = D_MODEL
    bf = lambda w: w.astype(BF16)
    vec = lambda w: w.reshape(w.shape[0], 1, D)

    x = jnp.concatenate([x_prompt.reshape(N_CTX, D), x_sample.reshape(N_LAT, D)], axis=0)
    cond = jnp.concatenate([c_ctx[None, :], c, jnp.zeros((COND_ROWS - 1 - DEC_BATCH, D), F32)], axis=0)
    mod = _ada_call(cond, ada_w, ada_b).reshape(DEPTH, COND_ROWS, 1, 6 * D)

    head_of = jnp.arange(D) // HEAD_DIM
    e = (head_of[:, None] == jnp.arange(LANES)[None, :]).astype(BF16)
    W = dict(
        mu=rwkv_mu, w_r=bf(rwkv_w_r), w_k=bf(rwkv_w_k), w_v=bf(rwkv_w_v),
        g1=bf(rwkv_g1), g2=bf(rwkv_g2),
        w0=rwkv_w0,
        w1=bf(jnp.concatenate([rwkv_w1[:, 0], rwkv_w1[:, 1]], axis=-1)),
        w2=bf(jnp.concatenate([rwkv_w2[:, 0], rwkv_w2[:, 1]], axis=-2)),
        a0=rwkv_a0,
        a1=bf(jnp.concatenate([rwkv_a1[:, 0], rwkv_a1[:, 1]], axis=-1)),
        a2=bf(jnp.concatenate([rwkv_a2[:, 0], rwkv_a2[:, 1]], axis=-2)),
        v0=vec(rwkv_v0), v1=bf(_pad_cols(rwkv_v1, LANES)), v2=bf(_pad_rows(rwkv_v2, LANES)),
        k_a=vec(rwkv_k_a), r_k=rwkv_r_k.reshape(N_RWKV, 1, D), e=e, et=e.T,
    )
    n1 = vec(norm1_g)
    n2 = vec(norm2_g)
    w_pair = bf(sgu_w_s.reshape(-1, N_PAIRS, 2, CHUNK, CHUNK).transpose(0, 1, 3, 2, 4)
                .reshape(-1, N_PAIRS, CHUNK, 2 * CHUNK))
    bmap = jnp.repeat(jnp.swapaxes(sgu_b_s, 1, 2), D // SGU_GROUPS, axis=2)
    sgu_w_in_b, sgu_w_out_b = bf(sgu_w_in), bf(sgu_w_out)
    mlp_w1_b, mlp_w2_b = bf(mlp_w1), bf(mlp_w2)
    w_o_b = bf(rwkv_w_o)

    s_lat = state_wkv.transpose(0, 1, 2, 4, 3, 5).reshape(DEC_BATCH, N_RWKV, 2, HEAD_DIM, D)
    s_init = jnp.concatenate([jnp.zeros((BATCH, N_RWKV, 2, HEAD_DIM, D), F32), s_lat], axis=0)

    v_first = None
    finals = []
    for l in range(DEPTH):
        i = l // 2
        if l % 2 == 0:
            x = _sgu_call(x, mod[l], n1[l], sgu_w_in_b[i], vec(sgu_ln_g)[i], vec(sgu_ln_b)[i],
                          w_pair[i], bmap[i], sgu_w_out_b[i])
        else:
            r, k, v, g, bv, lw, a = _proj_call(x, mod[l], n1[l], W, i, v_first)
            if v_first is None:
                v_first = v
            y, s_fin = _wkv_call(r, k, v, lw, a, vec(rwkv_k_k)[i], W['k_a'][i], s_init[:, i])
            finals.append(s_fin[:BATCH])
            x = _post_call(x, mod[l], y, bv, g, vec(rwkv_ln_g)[i], vec(rwkv_ln_b)[i], W['e'], W['et'],
                           w_o_b[i])
        x = _mlp_call(x, mod[l], n2[l], mlp_w1_b[l], mlp_w2_b[l])

    y_prompt = _final_call(x, final_g.reshape(1, D), 0, N_CTX).reshape(BATCH, SEQ, D)
    y_sample = _final_call(x, final_g.reshape(1, D), N_CTX, N_LAT).reshape(DEC_BATCH, DEC_SEQ, D)
    new_state = jnp.stack(finals, axis=1).reshape(BATCH, N_RWKV, 2, HEAD_DIM, N_HEADS, HEAD_DIM)
    new_state = new_state.transpose(0, 1, 2, 4, 3, 5)
    return (y_prompt, y_sample, new_state)
```

```python
import functools

import jax
import jax.numpy as jnp
from jax import lax
from jax.experimental import pallas as pl
from jax.experimental.pallas import tpu as pltpu

D_MODEL = 1024
BATCH = 16
SEQ = 256
DEPTH = 4
DEC_BATCH = 8
DEC_SEQ = 2048
N_RWKV = DEPTH // 2
CHUNK = 128
SGU_GROUPS = 16
HEAD_DIM = 64
N_HEADS = D_MODEL // HEAD_DIM
LORA_V = 32
D_FF = 4 * D_MODEL
NORM_EPS = 1e-6
GN_EPS = 64e-5

LANES = 128
N_PAIRS = D_MODEL // LANES
N_CTX = BATCH * SEQ
N_LAT = DEC_BATCH * DEC_SEQ
N_TOK = N_CTX + N_LAT
N_SEQ = BATCH + DEC_BATCH
COND_ROWS = 16

WKV_T = 64
NC_CTX = SEQ // WKV_T
NC_LAT = DEC_SEQ // WKV_T
WKV_ITEMS_CTX = BATCH * 2 * NC_CTX
WKV_ITEMS = WKV_ITEMS_CTX + DEC_BATCH * 2 * NC_LAT

TM_SGU = 256
TM_MLP = 512
TM_PROJ = 128
TM_POST = 256
TM_NORM = 512
FF_CHUNK = 1024
ADA_TN = 1536
VMEM_LIMIT = 56 * 1024 * 1024

F32 = jnp.float32
BF16 = jnp.bfloat16


def _dot(a, b):
    return jnp.dot(a, b, preferred_element_type=F32)


def _dot_nt(a, b):
    return lax.dot_general(a, b, (((1,), (1,)), ((), ())), preferred_element_type=F32)


def _dot_tn(a, b):
    return lax.dot_general(a, b, (((0,), (0,)), ((), ())), preferred_element_type=F32)


def _split3(x):
    h1 = x.astype(BF16)
    r1 = x - h1.astype(F32)
    h2 = r1.astype(BF16)
    h3 = (r1 - h2.astype(F32)).astype(BF16)
    return h1, h2, h3


def _dot_exact_rhs(x, m):
    h1, h2, h3 = _split3(x)
    return _dot(h1, m) + _dot(h2, m) + _dot(h3, m)


def _dot_exact_lhs(m, x):
    h1, h2, h3 = _split3(x)
    return _dot(m, h1) + _dot(m, h2) + _dot(m, h3)


def _sigmoid(x):
    return 1.0 / (1.0 + jnp.exp(-x))


def _norm_mod(x, g, scale, shift):
    ms = jnp.mean(x * x, axis=-1, keepdims=True)
    return (x * lax.rsqrt(ms + NORM_EPS) * g) * (1.0 + scale) + shift


def _stack_heads(x, lo):
    return jnp.concatenate([jnp.where(lo, x, 0.0), jnp.where(lo, 0.0, x)], axis=0)


def _cond_row(i, tm):
    n_ctx_blocks = N_CTX // tm
    return jnp.where(i < n_ctx_blocks, 0, 1 + (i - n_ctx_blocks) // (DEC_SEQ // tm))


def _const_spec(shape):
    nd = len(shape)
    return pl.BlockSpec(shape, lambda *_: (0,) * nd, pipeline_mode=pl.Buffered(1))


def _mod_spec(tm):
    return pl.BlockSpec((None, 1, 6 * D_MODEL), lambda i: (_cond_row(i, tm), 0, 0))


def _params(n_axes=1):
    return pltpu.CompilerParams(
        dimension_semantics=("arbitrary",) * n_axes, vmem_limit_bytes=VMEM_LIMIT)


def _ada_kernel(cond_ref, w_ref, b_ref, o_ref):
    c = cond_ref[...]
    s = c * _sigmoid(c)
    o_ref[...] = _dot(s.astype(BF16), w_ref[...].astype(BF16)) + b_ref[...]


def _ada_call(cond, ada_w, ada_b):
    return pl.pallas_call(
        _ada_kernel,
        grid=(DEPTH, 6 * D_MODEL // ADA_TN),
        in_specs=[
            pl.BlockSpec((COND_ROWS, D_MODEL), lambda l, n: (0, 0)),
            pl.BlockSpec((None, D_MODEL, ADA_TN), lambda l, n: (l, 0, n)),
            pl.BlockSpec((None, 1, ADA_TN), lambda l, n: (l, 0, n)),
        ],
        out_specs=pl.BlockSpec((None, COND_ROWS, ADA_TN), lambda l, n: (l, 0, n)),
        out_shape=jax.ShapeDtypeStruct((DEPTH, COND_ROWS, 6 * D_MODEL), F32),
        compiler_params=_params(2),
        name="ada_mod",
    )(cond, ada_w, ada_b.reshape(DEPTH, 1, 6 * D_MODEL))


def _sgu_kernel(x_ref, mod_ref, ng_ref, win_ref, lng_ref, lnb_ref, wp_ref, bmap_ref, wout_ref, o_ref):
    x = x_ref[...]
    mod = mod_ref[...]
    shift, scale, gate = (mod[:, j * D_MODEL:(j + 1) * D_MODEL] for j in range(3))
    h = _norm_mod(x, ng_ref[...], scale, shift)
    z = _dot(h.astype(BF16), win_ref[...])
    z = z * (0.5 * (1.0 + jnp.tanh(0.7978845608028654 * (z + 0.044715 * (z * z * z)))))
    u = z[:, :D_MODEL]
    v = z[:, D_MODEL:]
    mu = jnp.mean(v, axis=-1, keepdims=True)
    dv = v - mu
    var = jnp.mean(dv * dv, axis=-1, keepdims=True)
    v = dv * lax.rsqrt(var + NORM_EPS) * lng_ref[...] + lnb_ref[...]
    lo = lax.broadcasted_iota(jnp.int32, (CHUNK, LANES), 1) < HEAD_DIM
    bmap = bmap_ref[...]
    rows = []
    for c in range(TM_SGU // CHUNK):
        vc = v[c * CHUNK:(c + 1) * CHUNK]
        cols = []
        for p in range(N_PAIRS):
            rhs = _stack_heads(vc[:, p * LANES:(p + 1) * LANES], lo).astype(BF16)
            cols.append(_dot(wp_ref[p], rhs))
        rows.append(jnp.concatenate(cols, axis=1) + bmap)
    vs = jnp.concatenate(rows, axis=0)
    m = _dot((u * vs).astype(BF16), wout_ref[...])
    o_ref[...] = x + gate * m


def _sgu_call(x, mod, ng, w_in, ln_g, ln_b, w_pair, bmap, w_out):
    row = pl.BlockSpec((TM_SGU, D_MODEL), lambda i: (i, 0))
    return pl.pallas_call(
        _sgu_kernel,
        grid=(N_TOK // TM_SGU,),
        in_specs=[
            row, _mod_spec(TM_SGU),
            _const_spec((1, D_MODEL)),
            _const_spec((D_MODEL, 2 * D_MODEL)),
            _const_spec((1, D_MODEL)), _const_spec((1, D_MODEL)),
            _const_spec((N_PAIRS, CHUNK, 2 * CHUNK)),
            _const_spec((CHUNK, D_MODEL)),
            _const_spec((D_MODEL, D_MODEL)),
        ],
        out_specs=row,
        out_shape=jax.ShapeDtypeStruct((N_TOK, D_MODEL), F32),
        compiler_params=_params(),
        name="sgu_layer",
    )(x, mod, ng, w_in, ln_g, ln_b, w_pair, bmap, w_out)


def _mlp_kernel(x_ref, mod_ref, ng_ref, w1_ref, w2_ref, o_ref):
    x = x_ref[...]
    mod = mod_ref[...]
    shift, scale, gate = (mod[:, j * D_MODEL:(j + 1) * D_MODEL] for j in range(3, 6))
    h = _norm_mod(x, ng_ref[...], scale, shift).astype(BF16)
    acc = jnp.zeros((TM_MLP, D_MODEL), F32)
    for c in range(D_FF // FF_CHUNK):
        a = _dot(h, w1_ref[:, c * FF_CHUNK:(c + 1) * FF_CHUNK])
        a = jnp.square(jnp.maximum(a, 0.0))
        acc = acc + _dot(a.astype(BF16), w2_ref[c * FF_CHUNK:(c + 1) * FF_CHUNK, :])
    o_ref[...] = x + gate * acc


def _mlp_call(x, mod, ng, w1, w2):
    row = pl.BlockSpec((TM_MLP, D_MODEL), lambda i: (i, 0))
    return pl.pallas_call(
        _mlp_kernel,
        grid=(N_TOK // TM_MLP,),
        in_specs=[row, _mod_spec(TM_MLP), _const_spec((1, D_MODEL)),
                  _const_spec((D_MODEL, D_FF)), _const_spec((D_FF, D_MODEL))],
        out_specs=row,
        out_shape=jax.ShapeDtypeStruct((N_TOK, D_MODEL), F32),
        compiler_params=_params(),
        name="mlp_layer",
    )(x, mod, ng, w1, w2)


def _proj_kernel(has_vmix, *refs):
    if has_vmix:
        (x_ref, xp_ref, xn_ref, mod_ref, ng_ref, mu_ref, wr_ref, wk_ref, wv_ref, g1_ref, g2_ref,
         w0_ref, w1_ref, w2_ref, a0_ref, a1_ref, a2_ref, ka_ref, rk_ref, e_ref, et_ref,
         vf_ref, v0_ref, v1_ref, v2_ref,
         r_out, k_out, v_out, g_out, bv_out, lw_out, a_out) = refs
    else:
        (x_ref, xp_ref, xn_ref, mod_ref, ng_ref, mu_ref, wr_ref, wk_ref, wv_ref, g1_ref, g2_ref,
         w0_ref, w1_ref, w2_ref, a0_ref, a1_ref, a2_ref, ka_ref, rk_ref, e_ref, et_ref,
         r_out, k_out, v_out, g_out, bv_out, lw_out, a_out) = refs
    tm = TM_PROJ
    i = pl.program_id(0)
    t0 = i * tm
    in_ctx = t0 < N_CTX
    pos = jnp.where(in_ctx, t0 & (SEQ - 1), (t0 - N_CTX) & (DEC_SEQ - 1))
    seq_len = jnp.where(in_ctx, SEQ, DEC_SEQ)
    has_prev = jnp.where(pos != 0, 1.0, 0.0)
    has_next = jnp.where(pos + tm != seq_len, 1.0, 0.0)

    mod = mod_ref[...]
    shift, scale = mod[:, :D_MODEL], mod[:, D_MODEL:2 * D_MODEL]
    ng = ng_ref[...]
    h = _norm_mod(x_ref[...], ng, scale, shift)
    h_prev = _norm_mod(xp_ref[7:8, :], ng, scale, shift) * has_prev
    h_next = _norm_mod(xn_ref[0:1, :], ng, scale, shift) * has_next
    rows = lax.broadcasted_iota(jnp.int32, (tm, D_MODEL), 0)
    h_dn = jnp.where(rows == 0, h_prev, pltpu.roll(h, 1, 0))
    h_up = jnp.where(rows == tm - 1, h_next, pltpu.roll(h, tm - 1, 0))
    xx = 0.5 * (h_dn + h_up) - h
    mu = mu_ref[...]
    xr, xw, xk, xv, xa, xg = ((h + xx * mu[j:j + 1, :]).astype(BF16) for j in range(6))

    r = _dot(xr, wr_ref[...])
    k = _dot(xk, wk_ref[...])
    v = _dot(xv, wv_ref[...])
    if has_vmix:
        mix = _sigmoid(v0_ref[...] + _dot(_dot(xv, v1_ref[...]).astype(BF16), v2_ref[...]))
        v = v + (vf_ref[...] - v) * mix
    g = _dot(_sigmoid(_dot(xg, g1_ref[...])).astype(BF16), g2_ref[...])

    lo = lax.broadcasted_iota(jnp.int32, (tm, LANES), 1) < HEAD_DIM
    th = jnp.tanh(_dot(xw, w1_ref[...]))
    al = _dot(xa, a1_ref[...])
    ka = ka_ref[...]
    ksum = jnp.zeros_like(k)
    for d in range(2):
        keep = lo if d == 0 else jnp.logical_not(lo)
        z = w0_ref[d:d + 1, :] + _dot(jnp.where(keep, th, 0.0).astype(BF16), w2_ref[...])
        lw_out[d] = -0.6065306597126334 * _sigmoid(z)
        a_d = _sigmoid(a0_ref[d:d + 1, :] + _dot(jnp.where(keep, al, 0.0).astype(BF16), a2_ref[...]))
        a_out[d] = a_d
        ksum = ksum + k * (1.0 + (a_d - 1.0) * ka)
    head_sum = _dot_exact_rhs(r * ksum * rk_ref[...], e_ref[...])
    bonus = _dot_exact_rhs(head_sum, et_ref[...])
    r_out[...] = r
    k_out[...] = k
    v_out[...] = v
    g_out[...] = g
    bv_out[...] = bonus * v


def _proj_call(x, mod, ng, W, i, v_first):
    tm = TM_PROJ
    nb8 = tm // 8
    has_vmix = v_first is not None
    row = pl.BlockSpec((tm, D_MODEL), lambda i: (i, 0))
    row2 = pl.BlockSpec((2, tm, D_MODEL), lambda i: (0, i, 0))
    prev = pl.BlockSpec((8, D_MODEL), lambda i: (jnp.maximum(i * nb8 - 1, 0), 0))
    nxt = pl.BlockSpec((8, D_MODEL), lambda i: (jnp.minimum((i + 1) * nb8, N_TOK // 8 - 1), 0))
    mat = _const_spec((D_MODEL, D_MODEL))
    vec = _const_spec((1, D_MODEL))
    vec2 = _const_spec((2, D_MODEL))
    down = _const_spec((D_MODEL, LANES))
    up = _const_spec((LANES, D_MODEL))
    in_specs = [row, prev, nxt, _mod_spec(tm), vec, _const_spec((6, D_MODEL)),
                mat, mat, mat, down, up,
                vec2, down, up, vec2, down, up, vec, vec, down, up]
    args = [x, x, x, mod, ng, W['mu'][i], W['w_r'][i], W['w_k'][i], W['w_v'][i], W['g1'][i], W['g2'][i],
            W['w0'][i], W['w1'][i], W['w2'][i], W['a0'][i], W['a1'][i], W['a2'][i],
            W['k_a'][i], W['r_k'][i], W['e'], W['et']]
    if has_vmix:
        in_specs += [row, vec, down, up]
        args += [v_first, W['v0'][i - 1], W['v1'][i - 1], W['v2'][i - 1]]
    tok = jax.ShapeDtypeStruct((N_TOK, D_MODEL), F32)
    tok2 = jax.ShapeDtypeStruct((2, N_TOK, D_MODEL), F32)
    return pl.pallas_call(
        functools.partial(_proj_kernel, has_vmix),
        grid=(N_TOK // tm,),
        in_specs=in_specs,
        out_specs=[row, row, row, row, row, row2, row2],
        out_shape=[tok, tok, tok, tok, tok, tok2, tok2],
        compiler_params=_params(),
        name="rwkv_proj",
    )(*args)


def _wkv_decode(i):
    in_ctx = i < WKV_ITEMS_CTX
    il = jnp.maximum(i - WKV_ITEMS_CTX, 0)
    seq = jnp.where(in_ctx, i // (2 * NC_CTX), BATCH + il // (2 * NC_LAT))
    d = jnp.where(in_ctx, (i // NC_CTX) % 2, (il // NC_LAT) % 2)
    j = jnp.where(in_ctx, i % NC_CTX, il % NC_LAT)
    nc = jnp.where(in_ctx, NC_CTX, NC_LAT)
    cidx = jnp.where(d == 0, j, nc - 1 - j)
    base = jnp.where(in_ctx, (i // (2 * NC_CTX)) * NC_CTX, N_CTX // WKV_T + (il // (2 * NC_LAT)) * NC_LAT)
    return seq, d, j, nc, base + cidx


def _wkv_kernel(r_ref, k_ref, v_ref, lw_ref, a_ref, kkw_ref, ka_ref, s0_ref, y_ref, sf_ref, s_scr):
    T = WKV_T
    _, d, j, nc, _ = _wkv_decode(pl.program_id(0))
    lo = lax.broadcasted_iota(jnp.int32, (T, LANES), 1) < HEAD_DIM

    @pl.when(j == 0)
    def _():
        for p in range(N_PAIRS):
            s_scr[p] = _stack_heads(s0_ref[:, p * LANES:(p + 1) * LANES], lo)

    sgn = 1 - 2 * d
    tt = lax.broadcasted_iota(jnp.int32, (T, T), 0)
    ss = lax.broadcasted_iota(jnp.int32, (T, T), 1)
    cum = jnp.where(sgn * (tt - ss) + 1 > 0, 1.0, 0.0).astype(BF16)
    rho = lax.broadcasted_iota(jnp.int32, (2 * T, 4 * T), 0)
    gam = lax.broadcasted_iota(jnp.int32, (2 * T, 4 * T), 1)
    gmask = sgn * ((rho & (T - 1)) - (gam & (T - 1))) + (rho >> 6) > 0
    lvl = (lax.broadcasted_iota(jnp.int32, (T, 2 * T), 0)
           ^ (lax.broadcasted_iota(jnp.int32, (T, 2 * T), 1) & (T - 1)))
    same_head = ((lax.broadcasted_iota(jnp.int32, (2 * T, 2 * T), 0) >> 6)
                 == (lax.broadcasted_iota(jnp.int32, (2 * T, 2 * T), 1) >> 6))

    lw = lw_ref[...]
    logc = _dot_exact_lhs(cum, lw)
    tot = jnp.sum(lw, axis=0, keepdims=True)
    c = jnp.exp(logc)
    c_inv = jnp.exp(-logc)
    c_prev = jnp.exp(logc - lw)
    c_end = jnp.exp(tot - logc)
    c_tot = jnp.exp(tot)

    r = r_ref[...]
    k = k_ref[...]
    v = v_ref[...]
    a = a_ref[...]
    kk_raw = k * kkw_ref[...]
    kd = k * (1.0 + (a - 1.0) * ka_ref[...])

    def stk(x):
        return _stack_heads(x.astype(BF16), lo)

    pairs = range(N_PAIRS)
    sls = [slice(p * LANES, (p + 1) * LANES) for p in pairs]
    ar, bk, bkc = [], [], []
    for sl in sls:
        kq = kk_raw[:, sl]
        sq = kq * kq
        n_lo = jnp.sqrt(jnp.sum(jnp.where(lo, sq, 0.0), axis=-1, keepdims=True))
        n_hi = jnp.sqrt(jnp.sum(jnp.where(lo, 0.0, sq), axis=-1, keepdims=True))
        kk = kq / jnp.maximum(jnp.where(lo, n_lo, n_hi), 1e-12)
        beta = kk * a[:, sl]
        ar.append(jnp.concatenate([-kk * c_prev[:, sl], r[:, sl] * c[:, sl]], axis=0).astype(BF16))
        bk.append(jnp.concatenate([stk(beta * c_inv[:, sl]), stk(kd[:, sl] * c_inv[:, sl])], axis=0))
        bkc.append(jnp.concatenate([beta * c_end[:, sl], kd[:, sl] * c_end[:, sl]], axis=0).astype(BF16))
    gm = [jnp.where(gmask, _dot_nt(ar[p], bk[p]), 0.0) for p in pairs]
    uy0 = [_dot_nt(ar[p], s_scr[p].astype(BF16)) for p in pairs]
    v2 = [stk(v[:, sl]) for sl in sls]
    x = [uy0[p][:T] + _dot(gm[p][:T, 2 * T:].astype(BF16), v2[p]) for p in pairs]
    w = [jnp.where(lvl == 1, gm[p][:T, :2 * T], 0.0) for p in pairs]
    for lb in range(1, 6):
        p_off = [jnp.where((lvl >> lb) == 1, gm[p][:T, :2 * T], 0.0) for p in pairs]
        t1 = [p_off[p] + _dot(p_off[p].astype(BF16), stk(w[p])) for p in pairs]
        w = [w[p] + t1[p] + _dot(w[p].astype(BF16), stk(t1[p])) for p in pairs]
    u = [x[p] + _dot(w[p].astype(BF16), stk(x[p])) for p in pairs]
    for p in pairs:
        uv2 = jnp.concatenate([stk(u[p]), v2[p]], axis=0)
        y_ref[:, sls[p]] = uy0[p][T:] + _dot(gm[p][T:].astype(BF16), uv2)
    for p in pairs:
        uv = jnp.concatenate([u[p], v[:, sls[p]]], axis=0).astype(BF16)
        s_scr[p] = s_scr[p] * c_tot[:, sls[p]] + jnp.where(same_head, _dot_tn(uv, bkc[p]), 0.0)

    @pl.when(j == nc - 1)
    def _():
        for p in range(N_PAIRS):
            sf_ref[:, p * LANES:(p + 1) * LANES] = s_scr[p, :HEAD_DIM, :] + s_scr[p, HEAD_DIM:, :]


def _wkv_call(r, k, v, lw, a, kkw, ka, s0):
    tok = pl.BlockSpec((WKV_T, D_MODEL), lambda i: (_wkv_decode(i)[4], 0))
    tok_d = pl.BlockSpec((None, WKV_T, D_MODEL), lambda i: (_wkv_decode(i)[1], _wkv_decode(i)[4], 0))
    state = pl.BlockSpec((None, None, HEAD_DIM, D_MODEL),
                         lambda i: (_wkv_decode(i)[0], _wkv_decode(i)[1], 0, 0))
    vec = _const_spec((1, D_MODEL))
    return pl.pallas_call(
        _wkv_kernel,
        grid=(WKV_ITEMS,),
        in_specs=[tok, tok, tok, tok_d, tok_d, vec, vec, state],
        out_specs=[tok_d, state],
        out_shape=[jax.ShapeDtypeStruct((2, N_TOK, D_MODEL), F32),
                   jax.ShapeDtypeStruct((N_SEQ, 2, HEAD_DIM, D_MODEL), F32)],
        scratch_shapes=[pltpu.VMEM((N_PAIRS, 2 * HEAD_DIM, LANES), F32)],
        compiler_params=_params(),
        name="wkv_scan",
    )(r, k, v, lw, a, kkw, ka, s0)


def _post_kernel(x_ref, mod_ref, y_ref, bv_ref, g_ref, lng_ref, lnb_ref, e_ref, et_ref, wo_ref, o_ref):
    gate = mod_ref[...][:, 2 * D_MODEL:3 * D_MODEL]
    y = y_ref[0] + y_ref[1]
    e = e_ref[...]
    et = et_ref[...]
    inv_n = 1.0 / HEAD_DIM
    mu = _dot_exact_rhs(_dot_exact_rhs(y, e), et) * inv_n
    dv = y - mu
    var = _dot_exact_rhs(_dot_exact_rhs(dv * dv, e), et) * inv_n
    yn = dv * lax.rsqrt(var + GN_EPS) * lng_ref[...] + lnb_ref[...]
    yy = (yn + bv_ref[...]) * g_ref[...]
    o_ref[...] = x_ref[...] + gate * _dot(yy.astype(BF16), wo_ref[...])


def _post_call(x, mod, y, bv, g, ln_g, ln_b, e, et, w_o):
    tm = TM_POST
    row = pl.BlockSpec((tm, D_MODEL), lambda i: (i, 0))
    row2 = pl.BlockSpec((2, tm, D_MODEL), lambda i: (0, i, 0))
    vec = _const_spec((1, D_MODEL))
    return pl.pallas_call(
        _post_kernel,
        grid=(N_TOK // tm,),
        in_specs=[row, _mod_spec(tm), row2, row, row, vec, vec,
                  _const_spec((D_MODEL, LANES)), _const_spec((LANES, D_MODEL)),
                  _const_spec((D_MODEL, D_MODEL))],
        out_specs=row,
        out_shape=jax.ShapeDtypeStruct((N_TOK, D_MODEL), F32),
        compiler_params=_params(),
        name="rwkv_out",
    )(x, mod, y, bv, g, ln_g, ln_b, e, et, w_o)


def _final_kernel(x_ref, g_ref, o_ref):
    x = x_ref[...]
    ms = jnp.mean(x * x, axis=-1, keepdims=True)
    o_ref[...] = x * lax.rsqrt(ms + NORM_EPS) * g_ref[...]


def _final_call(x, g, row0, n_rows):
    tm = TM_NORM
    b0 = row0 // tm
    return pl.pallas_call(
        _final_kernel,
        grid=(n_rows // tm,),
        in_specs=[pl.BlockSpec((tm, D_MODEL), lambda i: (i + b0, 0)), _const_spec((1, D_MODEL))],
        out_specs=pl.BlockSpec((tm, D_MODEL), lambda i: (i, 0)),
        out_shape=jax.ShapeDtypeStruct((n_rows, D_MODEL), F32),
        compiler_params=_params(),
        name="final_norm",
    )(x, g)


def _pad_cols(w, n):
    return jnp.pad(w, [(0, 0)] * (w.ndim - 1) + [(0, n - w.shape[-1])])


def _pad_rows(w, n):
    return jnp.pad(w, [(0, 0)] * (w.ndim - 2) + [(0, n - w.shape[-2]), (0, 0)])


def kernel(x_prompt, x_sample, state_wkv, c, c_ctx, norm1_g, norm2_g, ada_w, ada_b, sgu_w_in, sgu_ln_g,
           sgu_ln_b, sgu_w_s, sgu_b_s, sgu_w_out, rwkv_mu, rwkv_w_r, rwkv_w_k, rwkv_w_v, rwkv_w_o,
           rwkv_w0, rwkv_w1, rwkv_w2, rwkv_a0, rwkv_a1, rwkv_a2, rwkv_v0, rwkv_v1, rwkv_v2, rwkv_g1,
           rwkv_g2, rwkv_k_k, rwkv_k_a, rwkv_r_k, rwkv_ln_g, rwkv_ln_b, mlp_w1, mlp_w2, final_g):
    D = D_MODEL
    bf = lambda w: w.astype(BF16)
    vec = lambda w: w.reshape(w.shape[0], 1, D)

    x = jnp.concatenate([x_prompt.reshape(N_CTX, D), x_sample.reshape(N_LAT, D)], axis=0)
    cond = jnp.concatenate([c_ctx[None, :], c, jnp.zeros((COND_ROWS - 1 - DEC_BATCH, D), F32)], axis=0)
    mod = _ada_call(cond, ada_w, ada_b).reshape(DEPTH, COND_ROWS, 1, 6 * D)

    head_of = jnp.arange(D) // HEAD_DIM
    e = (head_of[:, None] == jnp.arange(LANES)[None, :]).astype(BF16)
    W = dict(
        mu=rwkv_mu, w_r=bf(rwkv_w_r), w_k=bf(rwkv_w_k), w_v=bf(rwkv_w_v),
        g1=bf(rwkv_g1), g2=bf(rwkv_g2),
        w0=rwkv_w0,
        w1=bf(jnp.concatenate([rwkv_w1[:, 0], rwkv_w1[:, 1]], axis=-1)),
        w2=bf(jnp.concatenate([rwkv_w2[:, 0], rwkv_w2[:, 1]], axis=-2)),
        a0=rwkv_a0,
        a1=bf(jnp.concatenate([rwkv_a1[:, 0], rwkv_a1[:, 1]], axis=-1)),
        a2=bf(jnp.concatenate([rwkv_a2[:, 0], rwkv_a2[:, 1]], axis=-2)),
        v0=vec(rwkv_v0), v1=bf(_pad_cols(rwkv_v1, LANES)), v2=bf(_pad_rows(rwkv_v2, LANES)),
        k_a=vec(rwkv_k_a), r_k=rwkv_r_k.reshape(N_RWKV, 1, D), e=e, et=e.T,
    )
    n1 = vec(norm1_g)
    n2 = vec(norm2_g)
    w_pair = bf(sgu_w_s.reshape(-1, N_PAIRS, 2, CHUNK, CHUNK).transpose(0, 1, 3, 2, 4)
                .reshape(-1, N_PAIRS, CHUNK, 2 * CHUNK))
    bmap = jnp.repeat(jnp.swapaxes(sgu_b_s, 1, 2), D // SGU_GROUPS, axis=2)
    sgu_w_in_b, sgu_w_out_b = bf(sgu_w_in), bf(sgu_w_out)
    mlp_w1_b, mlp_w2_b = bf(mlp_w1), bf(mlp_w2)
    w_o_b = bf(rwkv_w_o)

    s_lat = state_wkv.transpose(0, 1, 2, 4, 3, 5).reshape(DEC_BATCH, N_RWKV, 2, HEAD_DIM, D)
    s_init = jnp.concatenate([jnp.zeros((BATCH, N_RWKV, 2, HEAD_DIM, D), F32), s_lat], axis=0)

    v_first = None
    finals = []
    for l in range(DEPTH):
        i = l // 2
        if l % 2 == 0:
            x = _sgu_call(x, mod[l], n1[l], sgu_w_in_b[i], vec(sgu_ln_g)[i], vec(sgu_ln_b)[i],
                          w_pair[i], bmap[i], sgu_w_out_b[i])
        else:
            r, k, v, g, bv, lw, a = _proj_call(x, mod[l], n1[l], W, i, v_first)
            if v_first is None:
                v_first = v
            y, s_fin = _wkv_call(r, k, v, lw, a, vec(rwkv_k_k)[i], W['k_a'][i], s_init[:, i])
            finals.append(s_fin[:BATCH])
            x = _post_call(x, mod[l], y, bv, g, vec(rwkv_ln_g)[i], vec(rwkv_ln_b)[i], W['e'], W['et'],
                           w_o_b[i])
        x = _mlp_call(x, mod[l], n2[l], mlp_w1_b[l], mlp_w2_b[l])

    y_prompt = _final_call(x, final_g.reshape(1, D), 0, N_CTX).reshape(BATCH, SEQ, D)
    y_sample = _final_call(x, final_g.reshape(1, D), N_CTX, N_LAT).reshape(DEC_BATCH, DEC_SEQ, D)
    new_state = jnp.stack(finals, axis=1).reshape(BATCH, N_RWKV, 2, HEAD_DIM, N_HEADS, HEAD_DIM)
    new_state = new_state.transpose(0, 1, 2, 4, 3, 5)
    return (y_prompt, y_sample, new_state)
```

```python
import functools

import numpy as np
import jax
import jax.numpy as jnp
from jax import lax
from jax.experimental import pallas as pl
from jax.experimental.pallas import tpu as pltpu

D_MODEL = 1024
BATCH = 16
SEQ = 256
DEPTH = 4
DEC_BATCH = 8
DEC_SEQ = 2048
N_RWKV = DEPTH // 2
CHUNK = 128
SGU_GROUPS = 16
HEAD_DIM = 64
N_HEADS = D_MODEL // HEAD_DIM
LORA_V = 32
D_FF = 4 * D_MODEL
NORM_EPS = 1e-6
GN_EPS = 64e-5

LANES = 128
N_PAIRS = D_MODEL // LANES
N_CTX = BATCH * SEQ
N_LAT = DEC_BATCH * DEC_SEQ
N_TOK = N_CTX + N_LAT
N_SEQ = BATCH + DEC_BATCH
COND_ROWS = 16

WKV_T = 64
NC_CTX = SEQ // WKV_T
NC_LAT = DEC_SEQ // WKV_T
WKV_STEPS_CTX = BATCH * NC_CTX
WKV_STEPS = WKV_STEPS_CTX + DEC_BATCH * NC_LAT

TM_SGU = 256
TM_MLP = 512
TM_PROJ = 256
TM_POST = 256
TM_NORM = 512
FF_CHUNK = 1024
ADA_TN = 1536
VMEM_LIMIT = 56 * 1024 * 1024

F32 = jnp.float32
BF16 = jnp.bfloat16


def _dot(a, b):
    return jnp.dot(a, b, preferred_element_type=F32)


def _dot_nt(a, b):
    return lax.dot_general(a, b, (((1,), (1,)), ((), ())), preferred_element_type=F32)


def _dot_tn(a, b):
    return lax.dot_general(a, b, (((0,), (0,)), ((), ())), preferred_element_type=F32)


def _split3(x):
    h1 = x.astype(BF16)
    r1 = x - h1.astype(F32)
    h2 = r1.astype(BF16)
    h3 = (r1 - h2.astype(F32)).astype(BF16)
    return h1, h2, h3


def _dot_exact_rhs(x, m):
    h1, h2, h3 = _split3(x)
    return _dot(h1, m) + _dot(h2, m) + _dot(h3, m)


def _dot_exact_lhs(m, x):
    h1, h2, h3 = _split3(x)
    return _dot(m, h1) + _dot(m, h2) + _dot(m, h3)


def _sigmoid(x):
    return 1.0 / (1.0 + jnp.exp(-x))


def _norm_mod(x, g, scale, shift):
    ms = jnp.mean(x * x, axis=-1, keepdims=True)
    return (x * lax.rsqrt(ms + NORM_EPS) * g) * (1.0 + scale) + shift


def _stack_heads(x, lo):
    return jnp.concatenate([jnp.where(lo, x, 0.0), jnp.where(lo, 0.0, x)], axis=0)


def _cond_row(i, tm):
    n_ctx_blocks = N_CTX // tm
    return jnp.where(i < n_ctx_blocks, 0, 1 + (i - n_ctx_blocks) // (DEC_SEQ // tm))


def _const_spec(shape):
    nd = len(shape)
    return pl.BlockSpec(shape, lambda *_: (0,) * nd, pipeline_mode=pl.Buffered(1))


def _mod_spec(tm):
    return pl.BlockSpec((None, 1, 6 * D_MODEL), lambda i: (_cond_row(i, tm), 0, 0))


def _params(n_axes=1):
    return pltpu.CompilerParams(
        dimension_semantics=("arbitrary",) * n_axes, vmem_limit_bytes=VMEM_LIMIT)


def _ada_kernel(cond_ref, w_ref, b_ref, o_ref):
    c = cond_ref[...]
    s = c * _sigmoid(c)
    o_ref[...] = _dot(s.astype(BF16), w_ref[...].astype(BF16)) + b_ref[...]


def _ada_call(cond, ada_w, ada_b):
    return pl.pallas_call(
        _ada_kernel,
        grid=(DEPTH, 6 * D_MODEL // ADA_TN),
        in_specs=[
            pl.BlockSpec((COND_ROWS, D_MODEL), lambda l, n: (0, 0)),
            pl.BlockSpec((None, D_MODEL, ADA_TN), lambda l, n: (l, 0, n)),
            pl.BlockSpec((None, 1, ADA_TN), lambda l, n: (l, 0, n)),
        ],
        out_specs=pl.BlockSpec((None, COND_ROWS, ADA_TN), lambda l, n: (l, 0, n)),
        out_shape=jax.ShapeDtypeStruct((DEPTH, COND_ROWS, 6 * D_MODEL), F32),
        compiler_params=_params(2),
        name="ada_mod",
    )(cond, ada_w, ada_b.reshape(DEPTH, 1, 6 * D_MODEL))


def _sgu_kernel(x_ref, mod_ref, ng_ref, win_ref, lng_ref, lnb_ref, wp_ref, bmap_ref, wout_ref, o_ref):
    x = x_ref[...]
    mod = mod_ref[...]
    shift, scale, gate = (mod[:, j * D_MODEL:(j + 1) * D_MODEL] for j in range(3))
    h = _norm_mod(x, ng_ref[...], scale, shift)
    z = _dot(h.astype(BF16), win_ref[...])
    z = z * (0.5 * (1.0 + jnp.tanh(0.7978845608028654 * (z + 0.044715 * (z * z * z)))))
    u = z[:, :D_MODEL]
    v = z[:, D_MODEL:]
    mu = jnp.mean(v, axis=-1, keepdims=True)
    dv = v - mu
    var = jnp.mean(dv * dv, axis=-1, keepdims=True)
    v = dv * lax.rsqrt(var + NORM_EPS) * lng_ref[...] + lnb_ref[...]
    lo = lax.broadcasted_iota(jnp.int32, (CHUNK, LANES), 1) < HEAD_DIM
    bmap = bmap_ref[...]
    rows = []
    for c in range(TM_SGU // CHUNK):
        vc = v[c * CHUNK:(c + 1) * CHUNK]
        cols = []
        for p in range(N_PAIRS):
            rhs = _stack_heads(vc[:, p * LANES:(p + 1) * LANES], lo).astype(BF16)
            cols.append(_dot(wp_ref[p], rhs))
        rows.append(jnp.concatenate(cols, axis=1) + bmap)
    vs = jnp.concatenate(rows, axis=0)
    m = _dot((u * vs).astype(BF16), wout_ref[...])
    o_ref[...] = x + gate * m


def _sgu_call(x, mod, ng, w_in, ln_g, ln_b, w_pair, bmap, w_out):
    row = pl.BlockSpec((TM_SGU, D_MODEL), lambda i: (i, 0))
    return pl.pallas_call(
        _sgu_kernel,
        grid=(N_TOK // TM_SGU,),
        in_specs=[
            row, _mod_spec(TM_SGU),
            _const_spec((1, D_MODEL)),
            _const_spec((D_MODEL, 2 * D_MODEL)),
            _const_spec((1, D_MODEL)), _const_spec((1, D_MODEL)),
            _const_spec((N_PAIRS, CHUNK, 2 * CHUNK)),
            _const_spec((CHUNK, D_MODEL)),
            _const_spec((D_MODEL, D_MODEL)),
        ],
        out_specs=row,
        out_shape=jax.ShapeDtypeStruct((N_TOK, D_MODEL), F32),
        compiler_params=_params(),
        name="sgu_layer",
    )(x, mod, ng, w_in, ln_g, ln_b, w_pair, bmap, w_out)


def _mlp_kernel(x_ref, mod_ref, ng_ref, w1_ref, w2_ref, o_ref):
    x = x_ref[...]
    mod = mod_ref[...]
    shift, scale, gate = (mod[:, j * D_MODEL:(j + 1) * D_MODEL] for j in range(3, 6))
    h = _norm_mod(x, ng_ref[...], scale, shift).astype(BF16)
    acc = jnp.zeros((TM_MLP, D_MODEL), F32)
    for c in range(D_FF // FF_CHUNK):
        a = _dot(h, w1_ref[:, c * FF_CHUNK:(c + 1) * FF_CHUNK])
        a = jnp.square(jnp.maximum(a, 0.0))
        acc = acc + _dot(a.astype(BF16), w2_ref[c * FF_CHUNK:(c + 1) * FF_CHUNK, :])
    o_ref[...] = x + gate * acc


def _mlp_call(x, mod, ng, w1, w2):
    row = pl.BlockSpec((TM_MLP, D_MODEL), lambda i: (i, 0))
    return pl.pallas_call(
        _mlp_kernel,
        grid=(N_TOK // TM_MLP,),
        in_specs=[row, _mod_spec(TM_MLP), _const_spec((1, D_MODEL)),
                  _const_spec((D_MODEL, D_FF)), _const_spec((D_FF, D_MODEL))],
        out_specs=row,
        out_shape=jax.ShapeDtypeStruct((N_TOK, D_MODEL), F32),
        compiler_params=_params(),
        name="mlp_layer",
    )(x, mod, ng, w1, w2)


def _proj_kernel(has_vmix, *refs):
    if has_vmix:
        (x_ref, xp_ref, xn_ref, mod_ref, ng_ref, mu_ref, wr_ref, wk_ref, wv_ref, g1_ref, g2_ref,
         w0_ref, w1_ref, w2_ref, a0_ref, a1_ref, a2_ref, ka_ref, rk_ref, e_ref, et_ref,
         vf_ref, v0_ref, v1_ref, v2_ref,
         r_out, k_out, v_out, g_out, bv_out, lw_out, a_out) = refs
    else:
        (x_ref, xp_ref, xn_ref, mod_ref, ng_ref, mu_ref, wr_ref, wk_ref, wv_ref, g1_ref, g2_ref,
         w0_ref, w1_ref, w2_ref, a0_ref, a1_ref, a2_ref, ka_ref, rk_ref, e_ref, et_ref,
         r_out, k_out, v_out, g_out, bv_out, lw_out, a_out) = refs
    tm = TM_PROJ
    i = pl.program_id(0)
    t0 = i * tm
    in_ctx = t0 < N_CTX
    pos = jnp.where(in_ctx, t0 & (SEQ - 1), (t0 - N_CTX) & (DEC_SEQ - 1))
    seq_len = jnp.where(in_ctx, SEQ, DEC_SEQ)
    has_prev = jnp.where(pos != 0, 1.0, 0.0)
    has_next = jnp.where(pos + tm != seq_len, 1.0, 0.0)

    mod = mod_ref[...]
    shift, scale = mod[:, :D_MODEL], mod[:, D_MODEL:2 * D_MODEL]
    ng = ng_ref[...]
    h = _norm_mod(x_ref[...], ng, scale, shift)
    h_prev = _norm_mod(xp_ref[7:8, :], ng, scale, shift) * has_prev
    h_next = _norm_mod(xn_ref[0:1, :], ng, scale, shift) * has_next
    rows = lax.broadcasted_iota(jnp.int32, (tm, D_MODEL), 0)
    h_dn = jnp.where(rows == 0, h_prev, pltpu.roll(h, 1, 0))
    h_up = jnp.where(rows == tm - 1, h_next, pltpu.roll(h, tm - 1, 0))
    xx = 0.5 * (h_dn + h_up) - h
    mu = mu_ref[...]
    xr, xw, xk, xv, xa, xg = ((h + xx * mu[j:j + 1, :]).astype(BF16) for j in range(6))

    r = _dot(xr, wr_ref[...])
    k = _dot(xk, wk_ref[...])
    v = _dot(xv, wv_ref[...])
    if has_vmix:
        mix = _sigmoid(v0_ref[...] + _dot(_dot(xv, v1_ref[...]).astype(BF16), v2_ref[...]))
        v = v + (vf_ref[...] - v) * mix
    g = _dot(_sigmoid(_dot(xg, g1_ref[...])).astype(BF16), g2_ref[...])

    lo = lax.broadcasted_iota(jnp.int32, (tm, LANES), 1) < HEAD_DIM
    th = jnp.tanh(_dot(xw, w1_ref[...]))
    al = _dot(xa, a1_ref[...])
    ka = ka_ref[...]
    ksum = jnp.zeros_like(k)
    for d in range(2):
        keep = lo if d == 0 else jnp.logical_not(lo)
        z = w0_ref[d:d + 1, :] + _dot(jnp.where(keep, th, 0.0).astype(BF16), w2_ref[...])
        lw_out[d] = -0.6065306597126334 * _sigmoid(z)
        a_d = _sigmoid(a0_ref[d:d + 1, :] + _dot(jnp.where(keep, al, 0.0).astype(BF16), a2_ref[...]))
        a_out[d] = a_d
        ksum = ksum + k * (1.0 + (a_d - 1.0) * ka)
    head_sum = _dot_exact_rhs(r * ksum * rk_ref[...], e_ref[...])
    bonus = _dot_exact_rhs(head_sum, et_ref[...])
    r_out[...] = r
    k_out[...] = k
    v_out[...] = v
    g_out[...] = g
    bv_out[...] = bonus * v


def _proj_call(x, mod, ng, W, i, v_first):
    tm = TM_PROJ
    nb8 = tm // 8
    has_vmix = v_first is not None
    row = pl.BlockSpec((tm, D_MODEL), lambda i: (i, 0))
    row2 = pl.BlockSpec((2, tm, D_MODEL), lambda i: (0, i, 0))
    prev = pl.BlockSpec((8, D_MODEL), lambda i: (jnp.maximum(i * nb8 - 1, 0), 0))
    nxt = pl.BlockSpec((8, D_MODEL), lambda i: (jnp.minimum((i + 1) * nb8, N_TOK // 8 - 1), 0))
    mat = _const_spec((D_MODEL, D_MODEL))
    vec = _const_spec((1, D_MODEL))
    vec2 = _const_spec((2, D_MODEL))
    down = _const_spec((D_MODEL, LANES))
    up = _const_spec((LANES, D_MODEL))
    in_specs = [row, prev, nxt, _mod_spec(tm), vec, _const_spec((6, D_MODEL)),
                mat, mat, mat, down, up,
                vec2, down, up, vec2, down, up, vec, vec, down, up]
    args = [x, x, x, mod, ng, W['mu'][i], W['w_r'][i], W['w_k'][i], W['w_v'][i], W['g1'][i], W['g2'][i],
            W['w0'][i], W['w1'][i], W['w2'][i], W['a0'][i], W['a1'][i], W['a2'][i],
            W['k_a'][i], W['r_k'][i], W['e'], W['et']]
    if has_vmix:
        in_specs += [row, vec, down, up]
        args += [v_first, W['v0'][i - 1], W['v1'][i - 1], W['v2'][i - 1]]
    tok = jax.ShapeDtypeStruct((N_TOK, D_MODEL), F32)
    tok2 = jax.ShapeDtypeStruct((2, N_TOK, D_MODEL), F32)
    return pl.pallas_call(
        functools.partial(_proj_kernel, has_vmix),
        grid=(N_TOK // tm,),
        in_specs=in_specs,
        out_specs=[row, row, row, row, row, row2, row2],
        out_shape=[tok, tok, tok, tok, tok, tok2, tok2],
        compiler_params=_params(),
        name="rwkv_proj",
    )(*args)


def _wkv_decode(i):
    in_ctx = i < WKV_STEPS_CTX
    il = jnp.maximum(i - WKV_STEPS_CTX, 0)
    seq = jnp.where(in_ctx, i // NC_CTX, BATCH + il // NC_LAT)
    j = jnp.where(in_ctx, i % NC_CTX, il % NC_LAT)
    nc = jnp.where(in_ctx, NC_CTX, NC_LAT)
    base = jnp.where(in_ctx, (i // NC_CTX) * NC_CTX, N_CTX // WKV_T + (il // NC_LAT) * NC_LAT)
    return seq, j, nc, base


def _wkv_block(i, d):
    _, j, nc, base = _wkv_decode(i)
    return base + (j if d == 0 else nc - 1 - j)


def _wkv_masks():
    T = WKV_T
    t = np.arange(T)[:, None]
    s1 = np.arange(T)[None, :]
    s2 = np.arange(2 * T)[None, :] % T
    s4 = np.arange(4 * T)[None, :] % T
    cum, strict, incl, lvl = [], [], [], []
    for d in range(2):
        early = (lambda s: s < t) if d == 0 else (lambda s: s > t)
        cum.append(early(s1) | (s1 == t))
        strict.append(early(s2))
        incl.append(early(s4) | (s4 == t))
        lvl.append([early(s2) & (((t ^ s2) >> lb) == 1) for lb in range(6)])
    hd = np.arange(2 * T) // HEAD_DIM
    return (jnp.asarray(np.array(cum), BF16), jnp.asarray(np.array(strict), F32),
            jnp.asarray(np.array(incl), F32), jnp.asarray(np.array(lvl), F32),
            jnp.asarray(hd[:, None] == hd[None, :], F32))


def _wkv_kernel(rf_ref, kf_ref, vf_ref, lwf_ref, af_ref, rb_ref, kb_ref, vb_ref, lwb_ref, ab_ref,
                kkw_ref, ka_ref, cum_ref, strict_ref, incl_ref, lvl_ref, same_ref, s0_ref,
                yf_ref, yb_ref, sf_ref, s_scr):
    T = WKV_T
    _, j, nc, _ = _wkv_decode(pl.program_id(0))
    lo = lax.broadcasted_iota(jnp.int32, (T, LANES), 1) < HEAD_DIM
    pairs = range(N_PAIRS)
    sls = [slice(p * LANES, (p + 1) * LANES) for p in pairs]
    units = [(d, p) for d in range(2) for p in pairs]

    @pl.when(j == 0)
    def _():
        for d, p in units:
            s_scr[d, p] = _stack_heads(s0_ref[d, :, sls[p]], lo)

    def stk(x):
        return _stack_heads(x.astype(BF16), lo)

    kkw = kkw_ref[...]
    ka = ka_ref[...]
    ar, bk, bkc, v2, vv, c_tot = {}, {}, {}, {}, {}, {}
    for d, (r_ref, k_ref, v_ref, lw_ref, a_ref) in enumerate(
            ((rf_ref, kf_ref, vf_ref, lwf_ref, af_ref), (rb_ref, kb_ref, vb_ref, lwb_ref, ab_ref))):
        lw = lw_ref[...]
        logc = _dot_exact_lhs(cum_ref[d], lw)
        tot = jnp.sum(lw, axis=0, keepdims=True)
        c = jnp.exp(logc)
        c_inv = jnp.exp(-logc)
        c_prev = jnp.exp(logc - lw)
        c_end = jnp.exp(tot - logc)
        c_tot[d] = jnp.exp(tot)
        r = r_ref[...]
        k = k_ref[...]
        a = a_ref[...]
        vv[d] = v_ref[...]
        kk_raw = k * kkw
        kd = k * (1.0 + (a - 1.0) * ka)
        for p in pairs:
            sl = sls[p]
            kq = kk_raw[:, sl]
            sq = kq * kq
            n_lo = jnp.sum(jnp.where(lo, sq, 0.0), axis=-1, keepdims=True)
            n_hi = jnp.sum(jnp.where(lo, 0.0, sq), axis=-1, keepdims=True)
            kk = kq * lax.rsqrt(jnp.maximum(jnp.where(lo, n_lo, n_hi), 1e-24))
            beta = kk * a[:, sl]
            ar[d, p] = jnp.concatenate([-kk * c_prev[:, sl], r[:, sl] * c[:, sl]], axis=0).astype(BF16)
            bk[d, p] = jnp.concatenate([stk(beta * c_inv[:, sl]), stk(kd[:, sl] * c_inv[:, sl])], axis=0)
            bkc[d, p] = jnp.concatenate([beta * c_end[:, sl], kd[:, sl] * c_end[:, sl]], axis=0).astype(BF16)
            v2[d, p] = stk(vv[d][:, sl])
    gm = {u: _dot_nt(ar[u], bk[u]) for u in units}
    uy0 = {u: _dot_nt(ar[u], s_scr[u].astype(BF16)) for u in units}
    x = {(d, p): uy0[d, p][:T] + _dot((gm[d, p][:T, 2 * T:] * strict_ref[d]).astype(BF16), v2[d, p])
         for d, p in units}
    w = {(d, p): gm[d, p][:T, :2 * T] * lvl_ref[d, 0] for d, p in units}
    for lb in range(1, 6):
        p_off = {(d, p): gm[d, p][:T, :2 * T] * lvl_ref[d, lb] for d, p in units}
        t1 = {u: p_off[u] + _dot(p_off[u].astype(BF16), stk(w[u])) for u in units}
        w = {u: w[u] + t1[u] + _dot(w[u].astype(BF16), stk(t1[u])) for u in units}
    uu = {u: x[u] + _dot(w[u].astype(BF16), stk(x[u])) for u in units}
    for d, p in units:
        y_ref = yf_ref if d == 0 else yb_ref
        uv2 = jnp.concatenate([stk(uu[d, p]), v2[d, p]], axis=0)
        y_ref[:, sls[p]] = uy0[d, p][T:] + _dot((gm[d, p][T:] * incl_ref[d]).astype(BF16), uv2)
    for d, p in units:
        uv = jnp.concatenate([uu[d, p], vv[d][:, sls[p]]], axis=0).astype(BF16)
        s_scr[d, p] = s_scr[d, p] * c_tot[d][:, sls[p]] + same_ref[...] * _dot_tn(uv, bkc[d, p])

    @pl.when(j == nc - 1)
    def _():
        for d, p in units:
            sf_ref[d, :, sls[p]] = s_scr[d, p, :HEAD_DIM, :] + s_scr[d, p, HEAD_DIM:, :]


def _wkv_call(r, k, v, lw, a, kkw, ka, s0):
    T = WKV_T
    tok = [pl.BlockSpec((T, D_MODEL), functools.partial(lambda d, i: (_wkv_block(i, d), 0), d))
           for d in range(2)]
    tok_d = [pl.BlockSpec((None, T, D_MODEL), functools.partial(lambda d, i: (d, _wkv_block(i, d), 0), d))
             for d in range(2)]
    state = pl.BlockSpec((None, 2, HEAD_DIM, D_MODEL), lambda i: (_wkv_decode(i)[0], 0, 0, 0))
    vec = _const_spec((1, D_MODEL))
    masks = _wkv_masks()
    y_shape = jax.ShapeDtypeStruct((N_TOK, D_MODEL), F32)
    return pl.pallas_call(
        _wkv_kernel,
        grid=(WKV_STEPS,),
        in_specs=[tok[0], tok[0], tok[0], tok_d[0], tok_d[0], tok[1], tok[1], tok[1], tok_d[1], tok_d[1],
                  vec, vec] + [_const_spec(m.shape) for m in masks] + [state],
        out_specs=[tok[0], tok[1], state],
        out_shape=[y_shape, y_shape, jax.ShapeDtypeStruct((N_SEQ, 2, HEAD_DIM, D_MODEL), F32)],
        scratch_shapes=[pltpu.VMEM((2, N_PAIRS, 2 * HEAD_DIM, LANES), F32)],
        compiler_params=_params(),
        name="wkv_scan",
    )(r, k, v, lw, a, r, k, v, lw, a, kkw, ka, *masks, s0)


def _post_kernel(x_ref, mod_ref, yf_ref, yb_ref, bv_ref, g_ref, lng_ref, lnb_ref, e_ref, et_ref, wo_ref,
                 o_ref):
    gate = mod_ref[...][:, 2 * D_MODEL:3 * D_MODEL]
    y = yf_ref[...] + yb_ref[...]
    e = e_ref[...]
    et = et_ref[...]
    inv_n = 1.0 / HEAD_DIM
    mu = _dot_exact_rhs(_dot_exact_rhs(y, e), et) * inv_n
    dv = y - mu
    var = _dot_exact_rhs(_dot_exact_rhs(dv * dv, e), et) * inv_n
    yn = dv * lax.rsqrt(var + GN_EPS) * lng_ref[...] + lnb_ref[...]
    yy = (yn + bv_ref[...]) * g_ref[...]
    o_ref[...] = x_ref[...] + gate * _dot(yy.astype(BF16), wo_ref[...])


def _post_call(x, mod, yf, yb, bv, g, ln_g, ln_b, e, et, w_o):
    tm = TM_POST
    row = pl.BlockSpec((tm, D_MODEL), lambda i: (i, 0))
    vec = _const_spec((1, D_MODEL))
    return pl.pallas_call(
        _post_kernel,
        grid=(N_TOK // tm,),
        in_specs=[row, _mod_spec(tm), row, row, row, row, vec, vec,
                  _const_spec((D_MODEL, LANES)), _const_spec((LANES, D_MODEL)),
                  _const_spec((D_MODEL, D_MODEL))],
        out_specs=row,
        out_shape=jax.ShapeDtypeStruct((N_TOK, D_MODEL), F32),
        compiler_params=_params(),
        name="rwkv_out",
    )(x, mod, yf, yb, bv, g, ln_g, ln_b, e, et, w_o)


def _final_kernel(x_ref, g_ref, o_ref):
    x = x_ref[...]
    ms = jnp.mean(x * x, axis=-1, keepdims=True)
    o_ref[...] = x * lax.rsqrt(ms + NORM_EPS) * g_ref[...]


def _final_call(x, g, row0, n_rows):
    tm = TM_NORM
    b0 = row0 // tm
    return pl.pallas_call(
        _final_kernel,
        grid=(n_rows // tm,),
        in_specs=[pl.BlockSpec((tm, D_MODEL), lambda i: (i + b0, 0)), _const_spec((1, D_MODEL))],
        out_specs=pl.BlockSpec((tm, D_MODEL), lambda i: (i, 0)),
        out_shape=jax.ShapeDtypeStruct((n_rows, D_MODEL), F32),
        compiler_params=_params(),
        name="final_norm",
    )(x, g)


def _pad_cols(w, n):
    return jnp.pad(w, [(0, 0)] * (w.ndim - 1) + [(0, n - w.shape[-1])])


def _pad_rows(w, n):
    return jnp.pad(w, [(0, 0)] * (w.ndim - 2) + [(0, n - w.shape[-2]), (0, 0)])


def kernel(x_prompt, x_sample, state_wkv, c, c_ctx, norm1_g, norm2_g, ada_w, ada_b, sgu_w_in, sgu_ln_g,
           sgu_ln_b, sgu_w_s, sgu_b_s, sgu_w_out, rwkv_mu, rwkv_w_r, rwkv_w_k, rwkv_w_v, rwkv_w_o,
           rwkv_w0, rwkv_w1, rwkv_w2, rwkv_a0, rwkv_a1, rwkv_a2, rwkv_v0, rwkv_v1, rwkv_v2, rwkv_g1,
           rwkv_g2, rwkv_k_k, rwkv_k_a, rwkv_r_k, rwkv_ln_g, rwkv_ln_b, mlp_w1, mlp_w2, final_g):
    D = D_MODEL
    bf = lambda w: w.astype(BF16)
    vec = lambda w: w.reshape(w.shape[0], 1, D)

    x = jnp.concatenate([x_prompt.reshape(N_CTX, D), x_sample.reshape(N_LAT, D)], axis=0)
    cond = jnp.concatenate([c_ctx[None, :], c, jnp.zeros((COND_ROWS - 1 - DEC_BATCH, D), F32)], axis=0)
    mod = _ada_call(cond, ada_w, ada_b).reshape(DEPTH, COND_ROWS, 1, 6 * D)

    head_of = jnp.arange(D) // HEAD_DIM
    e = (head_of[:, None] == jnp.arange(LANES)[None, :]).astype(BF16)
    W = dict(
        mu=rwkv_mu, w_r=bf(rwkv_w_r), w_k=bf(rwkv_w_k), w_v=bf(rwkv_w_v),
        g1=bf(rwkv_g1), g2=bf(rwkv_g2),
        w0=rwkv_w0,
        w1=bf(jnp.concatenate([rwkv_w1[:, 0], rwkv_w1[:, 1]], axis=-1)),
        w2=bf(jnp.concatenate([rwkv_w2[:, 0], rwkv_w2[:, 1]], axis=-2)),
        a0=rwkv_a0,
        a1=bf(jnp.concatenate([rwkv_a1[:, 0], rwkv_a1[:, 1]], axis=-1)),
        a2=bf(jnp.concatenate([rwkv_a2[:, 0], rwkv_a2[:, 1]], axis=-2)),
        v0=vec(rwkv_v0), v1=bf(_pad_cols(rwkv_v1, LANES)), v2=bf(_pad_rows(rwkv_v2, LANES)),
        k_a=vec(rwkv_k_a), r_k=rwkv_r_k.reshape(N_RWKV, 1, D), e=e, et=e.T,
    )
    n1 = vec(norm1_g)
    n2 = vec(norm2_g)
    w_pair = bf(sgu_w_s.reshape(-1, N_PAIRS, 2, CHUNK, CHUNK).transpose(0, 1, 3, 2, 4)
                .reshape(-1, N_PAIRS, CHUNK, 2 * CHUNK))
    bmap = jnp.repeat(jnp.swapaxes(sgu_b_s, 1, 2), D // SGU_GROUPS, axis=2)
    sgu_w_in_b, sgu_w_out_b = bf(sgu_w_in), bf(sgu_w_out)
    mlp_w1_b, mlp_w2_b = bf(mlp_w1), bf(mlp_w2)
    w_o_b = bf(rwkv_w_o)

    s_lat = state_wkv.transpose(0, 1, 2, 4, 3, 5).reshape(DEC_BATCH, N_RWKV, 2, HEAD_DIM, D)
    s_init = jnp.concatenate([jnp.zeros((BATCH, N_RWKV, 2, HEAD_DIM, D), F32), s_lat], axis=0)

    v_first = None
    finals = []
    for l in range(DEPTH):
        i = l // 2
        if l % 2 == 0:
            x = _sgu_call(x, mod[l], n1[l], sgu_w_in_b[i], vec(sgu_ln_g)[i], vec(sgu_ln_b)[i],
                          w_pair[i], bmap[i], sgu_w_out_b[i])
        else:
            r, k, v, g, bv, lw, a = _proj_call(x, mod[l], n1[l], W, i, v_first)
            if v_first is None:
                v_first = v
            yf, yb, s_fin = _wkv_call(r, k, v, lw, a, vec(rwkv_k_k)[i], W['k_a'][i], s_init[:, i])
            finals.append(s_fin[:BATCH])
            x = _post_call(x, mod[l], yf, yb, bv, g, vec(rwkv_ln_g)[i], vec(rwkv_ln_b)[i], W['e'], W['et'],
                           w_o_b[i])
        x = _mlp_call(x, mod[l], n2[l], mlp_w1_b[l], mlp_w2_b[l])

    y_prompt = _final_call(x, final_g.reshape(1, D), 0, N_CTX).reshape(BATCH, SEQ, D)
    y_sample = _final_call(x, final_g.reshape(1, D), N_CTX, N_LAT).reshape(DEC_BATCH, DEC_SEQ, D)
    new_state = jnp.stack(finals, axis=1).reshape(BATCH, N_RWKV, 2, HEAD_DIM, N_HEADS, HEAD_DIM)
    new_state = new_state.transpose(0, 1, 2, 4, 3, 5)
    return (y_prompt, y_sample, new_state)
```

```python
import functools

import numpy as np
import jax
import jax.numpy as jnp
from jax import lax
from jax.experimental import pallas as pl
from jax.experimental.pallas import tpu as pltpu

D_MODEL = 1024
BATCH = 16
SEQ = 256
DEPTH = 4
DEC_BATCH = 8
DEC_SEQ = 2048
N_RWKV = DEPTH // 2
CHUNK = 128
SGU_GROUPS = 16
HEAD_DIM = 64
N_HEADS = D_MODEL // HEAD_DIM
LORA_V = 32
D_FF = 4 * D_MODEL
NORM_EPS = 1e-6
GN_EPS = 64e-5

LANES = 128
N_PAIRS = D_MODEL // LANES
N_CTX = BATCH * SEQ
N_LAT = DEC_BATCH * DEC_SEQ
N_TOK = N_CTX + N_LAT
N_SEQ = BATCH + DEC_BATCH
COND_ROWS = 16

WKV_T = 64
WKV_NS = 2
NC_CTX = SEQ // WKV_T
NC_LAT = DEC_SEQ // WKV_T
WKV_STEPS_CTX = (BATCH // WKV_NS) * NC_CTX
WKV_STEPS = WKV_STEPS_CTX + (DEC_BATCH // WKV_NS) * NC_LAT

N_GROUPED = N_TOK // (WKV_NS * WKV_T)
TM_RWKV = 256
CHUNKS_PER_BLOCK = TM_RWKV // WKV_T

TM_SGU = 256
TM_MLP = 512
TM_NORM = 512
FF_CHUNK = 1024
ADA_TN = 1536
VMEM_LIMIT = 56 * 1024 * 1024

F32 = jnp.float32
BF16 = jnp.bfloat16


def _dot(a, b):
    return jnp.dot(a, b, preferred_element_type=F32)


def _dot_nt(a, b):
    return lax.dot_general(a, b, (((1,), (1,)), ((), ())), preferred_element_type=F32)


def _dot_tn(a, b):
    return lax.dot_general(a, b, (((0,), (0,)), ((), ())), preferred_element_type=F32)


def _split2(x):
    h1 = x.astype(BF16)
    return h1, (x - h1.astype(F32)).astype(BF16)


def _dot_exact_rhs(x, m):
    h1, h2 = _split2(x)
    return _dot(h1, m) + _dot(h2, m)


def _dot_exact_lhs(m, x):
    h1, h2 = _split2(x)
    return _dot(m, h1) + _dot(m, h2)


def _sigmoid(x):
    return 1.0 / (1.0 + jnp.exp(-x))


def _norm_mod(x, g, scale, shift):
    ms = jnp.mean(x * x, axis=-1, keepdims=True)
    return (x * lax.rsqrt(ms + NORM_EPS) * g) * (1.0 + scale) + shift


def _stack_heads(x, lo):
    return jnp.concatenate([jnp.where(lo, x, 0.0), jnp.where(lo, 0.0, x)], axis=0)


def _cond_row(i, tm):
    n_ctx_blocks = N_CTX // tm
    return jnp.where(i < n_ctx_blocks, 0, 1 + (i - n_ctx_blocks) // (DEC_SEQ // tm))


def _const_spec(shape):
    nd = len(shape)
    return pl.BlockSpec(shape, lambda *_: (0,) * nd, pipeline_mode=pl.Buffered(1))


def _layer_spec(shape, l):
    nd = len(shape)
    return pl.BlockSpec((None,) + tuple(shape), lambda *_: (l,) + (0,) * nd, pipeline_mode=pl.Buffered(1))


def _mod_spec(tm, l):
    return pl.BlockSpec((None, None, 1, 6 * D_MODEL), lambda i: (l, _cond_row(i, tm), 0, 0))


def _grouped_index(i):
    n_ctx_blocks = N_CTX // TM_RWKV
    per_seq = DEC_SEQ // TM_RWKV
    in_ctx = i < n_ctx_blocks
    il = jnp.maximum(i - n_ctx_blocks, 0)
    seq_l = il // per_seq
    blk = jnp.where(in_ctx, i // WKV_NS, n_ctx_blocks // WKV_NS + (seq_l // WKV_NS) * per_seq + il % per_seq)
    slot = jnp.where(in_ctx, i % WKV_NS, seq_l % WKV_NS)
    return blk, slot


def _grouped_spec():
    return pl.BlockSpec((CHUNKS_PER_BLOCK, None, WKV_T, D_MODEL), lambda i: (*_grouped_index(i), 0, 0))


def _grouped_spec2():
    return pl.BlockSpec((2, CHUNKS_PER_BLOCK, None, WKV_T, D_MODEL), lambda i: (0, *_grouped_index(i), 0, 0))


def _params(n_axes=1):
    return pltpu.CompilerParams(
        dimension_semantics=("arbitrary",) * n_axes, vmem_limit_bytes=VMEM_LIMIT)


def _ada_kernel(cond_ref, w_ref, b_ref, o_ref):
    c = cond_ref[...]
    s = c * _sigmoid(c)
    o_ref[...] = _dot(s.astype(BF16), w_ref[...].astype(BF16)) + b_ref[...]


def _ada_call(cond, ada_w, ada_b):
    return pl.pallas_call(
        _ada_kernel,
        grid=(DEPTH, 6 * D_MODEL // ADA_TN),
        in_specs=[
            pl.BlockSpec((COND_ROWS, D_MODEL), lambda l, n: (0, 0)),
            pl.BlockSpec((None, D_MODEL, ADA_TN), lambda l, n: (l, 0, n)),
            pl.BlockSpec((None, 1, ADA_TN), lambda l, n: (l, 0, n)),
        ],
        out_specs=pl.BlockSpec((None, COND_ROWS, ADA_TN), lambda l, n: (l, 0, n)),
        out_shape=jax.ShapeDtypeStruct((DEPTH, COND_ROWS, 6 * D_MODEL), F32),
        compiler_params=_params(2),
        name="ada_mod",
    )(cond, ada_w, ada_b.reshape(DEPTH, 1, 6 * D_MODEL))


def _sgu_kernel(split_input, *refs):
    if split_input:
        xa_ref, xb_ref, mod_ref, ng_ref, win_ref, lng_ref, lnb_ref, wp_ref, bmap_ref, wout_ref, o_ref = refs
        x = jnp.where(pl.program_id(0) < N_CTX // TM_SGU, xa_ref[...], xb_ref[...])
    else:
        x_ref, mod_ref, ng_ref, win_ref, lng_ref, lnb_ref, wp_ref, bmap_ref, wout_ref, o_ref = refs
        x = x_ref[...]
    mod = mod_ref[...]
    shift, scale, gate = (mod[:, j * D_MODEL:(j + 1) * D_MODEL] for j in range(3))
    h = _norm_mod(x, ng_ref[...], scale, shift)
    z = _dot(h.astype(BF16), win_ref[...])
    z = z * (0.5 * (1.0 + jnp.tanh(0.7978845608028654 * (z + 0.044715 * (z * z * z)))))
    u = z[:, :D_MODEL]
    v = z[:, D_MODEL:]
    mu = jnp.mean(v, axis=-1, keepdims=True)
    dv = v - mu
    var = jnp.mean(dv * dv, axis=-1, keepdims=True)
    v = dv * lax.rsqrt(var + NORM_EPS) * lng_ref[...] + lnb_ref[...]
    lo = lax.broadcasted_iota(jnp.int32, (CHUNK, LANES), 1) < HEAD_DIM
    bmap = bmap_ref[...]
    rows = []
    for c in range(TM_SGU // CHUNK):
        vc = v[c * CHUNK:(c + 1) * CHUNK]
        cols = []
        for p in range(N_PAIRS):
            rhs = _stack_heads(vc[:, p * LANES:(p + 1) * LANES], lo).astype(BF16)
            cols.append(_dot(wp_ref[p], rhs))
        rows.append(jnp.concatenate(cols, axis=1) + bmap)
    vs = jnp.concatenate(rows, axis=0)
    m = _dot((u * vs).astype(BF16), wout_ref[...])
    o_ref[...] = x + gate * m


def _sgu_call(xs, mod, l, ng, w_in, ln_g, ln_b, w_pair, bmap, w_out):
    i = l // 2
    tm = TM_SGU
    row = pl.BlockSpec((tm, D_MODEL), lambda b: (b, 0))
    if len(xs) == 2:
        n_ctx_blocks = N_CTX // tm
        x_specs = [pl.BlockSpec((tm, D_MODEL), lambda b: (jnp.minimum(b, n_ctx_blocks - 1), 0)),
                   pl.BlockSpec((tm, D_MODEL), lambda b: (jnp.maximum(b - n_ctx_blocks, 0), 0))]
    else:
        x_specs = [row]
    return pl.pallas_call(
        functools.partial(_sgu_kernel, len(xs) == 2),
        grid=(N_TOK // tm,),
        in_specs=x_specs + [
            _mod_spec(tm, l),
            _layer_spec((1, D_MODEL), l),
            _layer_spec((D_MODEL, 2 * D_MODEL), i),
            _layer_spec((1, D_MODEL), i), _layer_spec((1, D_MODEL), i),
            _layer_spec((N_PAIRS, CHUNK, 2 * CHUNK), i),
            _layer_spec((CHUNK, D_MODEL), i),
            _layer_spec((D_MODEL, D_MODEL), i),
        ],
        out_specs=row,
        out_shape=jax.ShapeDtypeStruct((N_TOK, D_MODEL), F32),
        compiler_params=_params(),
        name="sgu_layer",
    )(*xs, mod, ng, w_in, ln_g, ln_b, w_pair, bmap, w_out)


def _mlp_kernel(x_ref, mod_ref, ng_ref, w1_ref, w2_ref, o_ref):
    x = x_ref[...]
    mod = mod_ref[...]
    shift, scale, gate = (mod[:, j * D_MODEL:(j + 1) * D_MODEL] for j in range(3, 6))
    h = _norm_mod(x, ng_ref[...], scale, shift).astype(BF16)
    acc = jnp.zeros((TM_MLP, D_MODEL), F32)
    for c in range(D_FF // FF_CHUNK):
        a = _dot(h, w1_ref[:, c * FF_CHUNK:(c + 1) * FF_CHUNK])
        a = jnp.square(jnp.maximum(a, 0.0))
        acc = acc + _dot(a.astype(BF16), w2_ref[c * FF_CHUNK:(c + 1) * FF_CHUNK, :])
    o_ref[...] = x + gate * acc


def _mlp_call(x, mod, l, ng, w1, w2):
    row = pl.BlockSpec((TM_MLP, D_MODEL), lambda i: (i, 0))
    return pl.pallas_call(
        _mlp_kernel,
        grid=(N_TOK // TM_MLP,),
        in_specs=[row, _mod_spec(TM_MLP, l), _layer_spec((1, D_MODEL), l),
                  _layer_spec((D_MODEL, D_FF), l), _layer_spec((D_FF, D_MODEL), l)],
        out_specs=row,
        out_shape=jax.ShapeDtypeStruct((N_TOK, D_MODEL), F32),
        compiler_params=_params(),
        name="mlp_layer",
    )(x, mod, ng, w1, w2)


def _proj_kernel(has_vmix, *refs):
    if has_vmix:
        (x_ref, xp_ref, xn_ref, mod_ref, ng_ref, mu_ref, wr_ref, wk_ref, wv_ref, g1_ref, g2_ref,
         w0_ref, w1_ref, w2_ref, a0_ref, a1_ref, a2_ref, ka_ref, rk_ref, e_ref, et_ref,
         vf_ref, v0_ref, v1_ref, v2_ref,
         r_out, k_out, v_out, g_out, bv_out, lw_out, a_out) = refs
    else:
        (x_ref, xp_ref, xn_ref, mod_ref, ng_ref, mu_ref, wr_ref, wk_ref, wv_ref, g1_ref, g2_ref,
         w0_ref, w1_ref, w2_ref, a0_ref, a1_ref, a2_ref, ka_ref, rk_ref, e_ref, et_ref,
         r_out, k_out, v_out, g_out, bv_out, lw_out, a_out) = refs
    tm = TM_RWKV
    grouped = (CHUNKS_PER_BLOCK, WKV_T, D_MODEL)
    i = pl.program_id(0)
    t0 = i * tm
    in_ctx = t0 < N_CTX
    pos = jnp.where(in_ctx, t0 & (SEQ - 1), (t0 - N_CTX) & (DEC_SEQ - 1))
    seq_len = jnp.where(in_ctx, SEQ, DEC_SEQ)
    has_prev = jnp.where(pos != 0, 1.0, 0.0)
    has_next = jnp.where(pos + tm != seq_len, 1.0, 0.0)

    mod = mod_ref[...]
    shift, scale = mod[:, :D_MODEL], mod[:, D_MODEL:2 * D_MODEL]
    ng = ng_ref[...]
    h = _norm_mod(x_ref[...], ng, scale, shift)
    h_prev = _norm_mod(xp_ref[7:8, :], ng, scale, shift) * has_prev
    h_next = _norm_mod(xn_ref[0:1, :], ng, scale, shift) * has_next
    rows = lax.broadcasted_iota(jnp.int32, (tm, D_MODEL), 0)
    h_dn = jnp.where(rows == 0, h_prev, pltpu.roll(h, 1, 0))
    h_up = jnp.where(rows == tm - 1, h_next, pltpu.roll(h, tm - 1, 0))
    xx = 0.5 * (h_dn + h_up) - h
    mu = mu_ref[...]
    xr, xw, xk, xv, xa, xg = ((h + xx * mu[j:j + 1, :]).astype(BF16) for j in range(6))

    r = _dot(xr, wr_ref[...])
    k = _dot(xk, wk_ref[...])
    v = _dot(xv, wv_ref[...])
    if has_vmix:
        mix = _sigmoid(v0_ref[...] + _dot(_dot(xv, v1_ref[...]).astype(BF16), v2_ref[...]))
        v = v + (vf_ref[...].reshape(tm, D_MODEL) - v) * mix
    g = _dot(_sigmoid(_dot(xg, g1_ref[...])).astype(BF16), g2_ref[...])

    lo = lax.broadcasted_iota(jnp.int32, (tm, LANES), 1) < HEAD_DIM
    th = jnp.tanh(_dot(xw, w1_ref[...]))
    al = _dot(xa, a1_ref[...])
    ka = ka_ref[...]
    ksum = jnp.zeros_like(k)
    for d in range(2):
        keep = lo if d == 0 else jnp.logical_not(lo)
        z = w0_ref[d:d + 1, :] + _dot(jnp.where(keep, th, 0.0).astype(BF16), w2_ref[...])
        lw_out[d] = (-0.6065306597126334 * _sigmoid(z)).reshape(grouped)
        a_d = _sigmoid(a0_ref[d:d + 1, :] + _dot(jnp.where(keep, al, 0.0).astype(BF16), a2_ref[...]))
        a_out[d] = a_d.reshape(grouped)
        ksum = ksum + k * (1.0 + (a_d - 1.0) * ka)
    head_sum = _dot_exact_rhs(r * ksum * rk_ref[...], e_ref[...])
    bonus = _dot_exact_rhs(head_sum, et_ref[...])
    r_out[...] = r.reshape(grouped)
    k_out[...] = k.reshape(grouped)
    v_out[...] = v.reshape(grouped)
    g_out[...] = g.reshape(grouped)
    bv_out[...] = (bonus * v).reshape(grouped)


def _proj_call(x, mod, l, ng, W, v_first):
    i = l // 2
    tm = TM_RWKV
    nb8 = tm // 8
    has_vmix = v_first is not None
    row = pl.BlockSpec((tm, D_MODEL), lambda b: (b, 0))
    prev = pl.BlockSpec((8, D_MODEL), lambda b: (jnp.maximum(b * nb8 - 1, 0), 0))
    nxt = pl.BlockSpec((8, D_MODEL), lambda b: (jnp.minimum((b + 1) * nb8, N_TOK // 8 - 1), 0))
    mat = _layer_spec((D_MODEL, D_MODEL), i)
    vec = _layer_spec((1, D_MODEL), i)
    vec2 = _layer_spec((2, D_MODEL), i)
    down = _layer_spec((D_MODEL, LANES), i)
    up = _layer_spec((LANES, D_MODEL), i)
    in_specs = [row, prev, nxt, _mod_spec(tm, l), _layer_spec((1, D_MODEL), l), _layer_spec((6, D_MODEL), i),
                mat, mat, mat, down, up,
                vec2, down, up, vec2, down, up, vec, vec,
                _const_spec((D_MODEL, LANES)), _const_spec((LANES, D_MODEL))]
    args = [x, x, x, mod, ng, W['mu'], W['w_r'], W['w_k'], W['w_v'], W['g1'], W['g2'],
            W['w0'], W['w1'], W['w2'], W['a0'], W['a1'], W['a2'], W['k_a'], W['r_k'], W['e'], W['et']]
    if has_vmix:
        in_specs += [_grouped_spec(), _layer_spec((1, D_MODEL), i - 1), _layer_spec((D_MODEL, LANES), i - 1),
                     _layer_spec((LANES, D_MODEL), i - 1)]
        args += [v_first, W['v0'], W['v1'], W['v2']]
    tok = jax.ShapeDtypeStruct((N_GROUPED, WKV_NS, WKV_T, D_MODEL), F32)
    tok2 = jax.ShapeDtypeStruct((2, N_GROUPED, WKV_NS, WKV_T, D_MODEL), F32)
    return pl.pallas_call(
        functools.partial(_proj_kernel, has_vmix),
        grid=(N_TOK // tm,),
        in_specs=in_specs,
        out_specs=[_grouped_spec()] * 5 + [_grouped_spec2()] * 2,
        out_shape=[tok] * 5 + [tok2] * 2,
        compiler_params=_params(),
        name="rwkv_proj",
    )(*args)


def _wkv_decode(i):
    in_ctx = i < WKV_STEPS_CTX
    il = jnp.maximum(i - WKV_STEPS_CTX, 0)
    grp = jnp.where(in_ctx, i // NC_CTX, BATCH // WKV_NS + il // NC_LAT)
    j = jnp.where(in_ctx, i % NC_CTX, il % NC_LAT)
    nc = jnp.where(in_ctx, NC_CTX, NC_LAT)
    base = jnp.where(in_ctx, (i // NC_CTX) * NC_CTX, WKV_STEPS_CTX + (il // NC_LAT) * NC_LAT)
    return grp, j, nc, base


def _wkv_entry(i, d):
    _, j, nc, base = _wkv_decode(i)
    return base + (j if d == 0 else nc - 1 - j)


def _wkv_masks():
    T = WKV_T
    t = np.arange(T)[:, None]
    s1 = np.arange(T)[None, :]
    s2 = np.arange(2 * T)[None, :] % T
    s4 = np.arange(4 * T)[None, :] % T
    cum, strict, incl, lvl = [], [], [], []
    for d in range(2):
        early = (lambda s: s < t) if d == 0 else (lambda s: s > t)
        cum.append(early(s1) | (s1 == t))
        strict.append(early(s2))
        incl.append(early(s4) | (s4 == t))
        lvl.append([early(s2) & (((t ^ s2) >> lb) == 1) for lb in range(6)])
    hd = np.arange(2 * T) // HEAD_DIM
    return (jnp.asarray(np.array(cum), BF16), jnp.asarray(np.array(strict), F32),
            jnp.asarray(np.array(incl), F32), jnp.asarray(np.array(lvl), F32),
            jnp.asarray(hd[:, None] == hd[None, :], F32))


def _wkv_kernel(rf_ref, kf_ref, vf_ref, lwf_ref, af_ref, rb_ref, kb_ref, vb_ref, lwb_ref, ab_ref,
                kkw_ref, ka_ref, cum_ref, strict_ref, incl_ref, lvl_ref, same_ref, s0_ref,
                yf_ref, yb_ref, sf_ref, s_scr):
    T = WKV_T
    _, j, nc, _ = _wkv_decode(pl.program_id(0))
    lo = lax.broadcasted_iota(jnp.int32, (T, LANES), 1) < HEAD_DIM
    pairs = range(N_PAIRS)
    sls = [slice(p * LANES, (p + 1) * LANES) for p in pairs]
    chains = [(q, d) for q in range(WKV_NS) for d in range(2)]
    units = [(q, d, p) for q, d in chains for p in pairs]

    @pl.when(j == 0)
    def _():
        for q, d, p in units:
            s_scr[q, d, p] = _stack_heads(s0_ref[q, d, :, sls[p]], lo)

    def stk(x):
        return _stack_heads(x.astype(BF16), lo)

    kkw = kkw_ref[...]
    ka = ka_ref[...]
    dir_refs = ((rf_ref, kf_ref, vf_ref, lwf_ref, af_ref), (rb_ref, kb_ref, vb_ref, lwb_ref, ab_ref))
    ar, bk, bkc, v2, vv, c_tot = {}, {}, {}, {}, {}, {}
    for q, d in chains:
        r_ref, k_ref, v_ref, lw_ref, a_ref = dir_refs[d]
        lw = lw_ref[q]
        logc = _dot_exact_lhs(cum_ref[d], lw)
        tot = jnp.sum(lw, axis=0, keepdims=True)
        c = jnp.exp(logc)
        c_inv = jnp.exp(-logc)
        c_prev = jnp.exp(logc - lw)
        c_end = jnp.exp(tot - logc)
        c_tot[q, d] = jnp.exp(tot)
        r = r_ref[q]
        k = k_ref[q]
        a = a_ref[q]
        vv[q, d] = v_ref[q]
        kk_raw = k * kkw
        kd = k * (1.0 + (a - 1.0) * ka)
        for p in pairs:
            sl = sls[p]
            kq = kk_raw[:, sl]
            sq = kq * kq
            n_lo = jnp.sum(jnp.where(lo, sq, 0.0), axis=-1, keepdims=True)
            n_hi = jnp.sum(jnp.where(lo, 0.0, sq), axis=-1, keepdims=True)
            kk = kq * lax.rsqrt(jnp.maximum(jnp.where(lo, n_lo, n_hi), 1e-24))
            beta = kk * a[:, sl]
            u = (q, d, p)
            ar[u] = jnp.concatenate([-kk * c_prev[:, sl], r[:, sl] * c[:, sl]], axis=0).astype(BF16)
            bk[u] = jnp.concatenate([stk(beta * c_inv[:, sl]), stk(kd[:, sl] * c_inv[:, sl])], axis=0)
            bkc[u] = jnp.concatenate([beta * c_end[:, sl], kd[:, sl] * c_end[:, sl]], axis=0).astype(BF16)
            v2[u] = stk(vv[q, d][:, sl])
    gm = {u: _dot_nt(ar[u], bk[u]) for u in units}
    uy0 = {u: _dot_nt(ar[u], s_scr[u].astype(BF16)) for u in units}
    x = {u: uy0[u][:T] + _dot((gm[u][:T, 2 * T:] * strict_ref[u[1]]).astype(BF16), v2[u]) for u in units}
    w = {u: gm[u][:T, :2 * T] * lvl_ref[u[1], 0] for u in units}
    for lb in range(1, 6):
        p_off = {u: gm[u][:T, :2 * T] * lvl_ref[u[1], lb] for u in units}
        t1 = {u: p_off[u] + _dot(p_off[u].astype(BF16), stk(w[u])) for u in units}
        w = {u: w[u] + t1[u] + _dot(w[u].astype(BF16), stk(t1[u])) for u in units}
    uu = {u: x[u] + _dot(w[u].astype(BF16), stk(x[u])) for u in units}
    for u in units:
        q, d, p = u
        y_ref = yf_ref if d == 0 else yb_ref
        uv2 = jnp.concatenate([stk(uu[u]), v2[u]], axis=0)
        y_ref[q, :, sls[p]] = uy0[u][T:] + _dot((gm[u][T:] * incl_ref[d]).astype(BF16), uv2)
    for u in units:
        q, d, p = u
        uv = jnp.concatenate([uu[u], vv[q, d][:, sls[p]]], axis=0).astype(BF16)
        s_scr[u] = s_scr[u] * c_tot[q, d][:, sls[p]] + same_ref[...] * _dot_tn(uv, bkc[u])

    @pl.when(j == nc - 1)
    def _():
        for q, d, p in units:
            sf_ref[q, d, :, sls[p]] = s_scr[q, d, p, :HEAD_DIM, :] + s_scr[q, d, p, HEAD_DIM:, :]


def _wkv_call(r, k, v, lw, a, kkw, ka, s0, i):
    blk = (WKV_NS, WKV_T, D_MODEL)
    tok = [pl.BlockSpec((None,) + blk, functools.partial(lambda d, b: (_wkv_entry(b, d), 0, 0, 0), d))
           for d in range(2)]
    tok_d = [pl.BlockSpec((None, None) + blk, functools.partial(lambda d, b: (d, _wkv_entry(b, d), 0, 0, 0), d))
             for d in range(2)]
    state = pl.BlockSpec((WKV_NS, 2, HEAD_DIM, D_MODEL), lambda b: (_wkv_decode(b)[0], 0, 0, 0))
    vec = _layer_spec((1, D_MODEL), i)
    masks = _wkv_masks()
    y_shape = jax.ShapeDtypeStruct((N_GROUPED,) + blk, F32)
    return pl.pallas_call(
        _wkv_kernel,
        grid=(WKV_STEPS,),
        in_specs=[tok[0], tok[0], tok[0], tok_d[0], tok_d[0], tok[1], tok[1], tok[1], tok_d[1], tok_d[1],
                  vec, vec] + [_const_spec(m.shape) for m in masks] + [state],
        out_specs=[tok[0], tok[1], state],
        out_shape=[y_shape, y_shape, jax.ShapeDtypeStruct((N_SEQ, 2, HEAD_DIM, D_MODEL), F32)],
        scratch_shapes=[pltpu.VMEM((WKV_NS, 2, N_PAIRS, 2 * HEAD_DIM, LANES), F32)],
        compiler_params=_params(),
        name="wkv_scan",
    )(r, k, v, lw, a, r, k, v, lw, a, kkw, ka, *masks, s0)


def _post_kernel(x_ref, mod_ref, yf_ref, yb_ref, bv_ref, g_ref, lng_ref, lnb_ref, e_ref, et_ref, wo_ref,
                 o_ref):
    rows = (TM_RWKV, D_MODEL)
    gate = mod_ref[...][:, 2 * D_MODEL:3 * D_MODEL]
    y = (yf_ref[...] + yb_ref[...]).reshape(rows)
    e = e_ref[...]
    et = et_ref[...]
    inv_n = 1.0 / HEAD_DIM
    mu = _dot_exact_rhs(_dot_exact_rhs(y, e), et) * inv_n
    dv = y - mu
    var = _dot_exact_rhs(_dot_exact_rhs(dv * dv, e), et) * inv_n
    yn = dv * lax.rsqrt(var + GN_EPS) * lng_ref[...] + lnb_ref[...]
    yy = (yn + bv_ref[...].reshape(rows)) * g_ref[...].reshape(rows)
    o_ref[...] = x_ref[...] + gate * _dot(yy.astype(BF16), wo_ref[...])


def _post_call(x, mod, l, yf, yb, bv, g, ln_g, ln_b, e, et, w_o):
    i = l // 2
    tm = TM_RWKV
    row = pl.BlockSpec((tm, D_MODEL), lambda b: (b, 0))
    grouped = _grouped_spec()
    vec = _layer_spec((1, D_MODEL), i)
    return pl.pallas_call(
        _post_kernel,
        grid=(N_TOK // tm,),
        in_specs=[row, _mod_spec(tm, l), grouped, grouped, grouped, grouped, vec, vec,
                  _const_spec((D_MODEL, LANES)), _const_spec((LANES, D_MODEL)),
                  _layer_spec((D_MODEL, D_MODEL), i)],
        out_specs=row,
        out_shape=jax.ShapeDtypeStruct((N_TOK, D_MODEL), F32),
        compiler_params=_params(),
        name="rwkv_out",
    )(x, mod, yf, yb, bv, g, ln_g, ln_b, e, et, w_o)


def _final_kernel(x_ref, g_ref, o_ref):
    x = x_ref[...]
    ms = jnp.mean(x * x, axis=-1, keepdims=True)
    o_ref[...] = x * lax.rsqrt(ms + NORM_EPS) * g_ref[...]


def _final_call(x, g, row0, n_rows):
    tm = TM_NORM
    b0 = row0 // tm
    return pl.pallas_call(
        _final_kernel,
        grid=(n_rows // tm,),
        in_specs=[pl.BlockSpec((tm, D_MODEL), lambda i: (i + b0, 0)), _const_spec((1, D_MODEL))],
        out_specs=pl.BlockSpec((tm, D_MODEL), lambda i: (i, 0)),
        out_shape=jax.ShapeDtypeStruct((n_rows, D_MODEL), F32),
        compiler_params=_params(),
        name="final_norm",
    )(x, g)


def _pad_cols(w, n):
    return jnp.pad(w, [(0, 0)] * (w.ndim - 1) + [(0, n - w.shape[-1])])


def _pad_rows(w, n):
    return jnp.pad(w, [(0, 0)] * (w.ndim - 2) + [(0, n - w.shape[-2]), (0, 0)])


def kernel(x_prompt, x_sample, state_wkv, c, c_ctx, norm1_g, norm2_g, ada_w, ada_b, sgu_w_in, sgu_ln_g,
           sgu_ln_b, sgu_w_s, sgu_b_s, sgu_w_out, rwkv_mu, rwkv_w_r, rwkv_w_k, rwkv_w_v, rwkv_w_o,
           rwkv_w0, rwkv_w1, rwkv_w2, rwkv_a0, rwkv_a1, rwkv_a2, rwkv_v0, rwkv_v1, rwkv_v2, rwkv_g1,
           rwkv_g2, rwkv_k_k, rwkv_k_a, rwkv_r_k, rwkv_ln_g, rwkv_ln_b, mlp_w1, mlp_w2, final_g):
    D = D_MODEL
    bf = lambda w: w.astype(BF16)
    vec = lambda w: w.reshape(w.shape[0], 1, D)

    cond = jnp.concatenate([c_ctx[None, :], c, jnp.zeros((COND_ROWS - 1 - DEC_BATCH, D), F32)], axis=0)
    mod = _ada_call(cond, ada_w, ada_b).reshape(DEPTH, COND_ROWS, 1, 6 * D)

    head_of = jnp.arange(D) // HEAD_DIM
    e = (head_of[:, None] == jnp.arange(LANES)[None, :]).astype(BF16)
    W = dict(
        mu=rwkv_mu, w_r=bf(rwkv_w_r), w_k=bf(rwkv_w_k), w_v=bf(rwkv_w_v),
        g1=bf(rwkv_g1), g2=bf(rwkv_g2),
        w0=rwkv_w0,
        w1=bf(jnp.concatenate([rwkv_w1[:, 0], rwkv_w1[:, 1]], axis=-1)),
        w2=bf(jnp.concatenate([rwkv_w2[:, 0], rwkv_w2[:, 1]], axis=-2)),
        a0=rwkv_a0,
        a1=bf(jnp.concatenate([rwkv_a1[:, 0], rwkv_a1[:, 1]], axis=-1)),
        a2=bf(jnp.concatenate([rwkv_a2[:, 0], rwkv_a2[:, 1]], axis=-2)),
        v0=vec(rwkv_v0), v1=bf(_pad_cols(rwkv_v1, LANES)), v2=bf(_pad_rows(rwkv_v2, LANES)),
        k_a=vec(rwkv_k_a), r_k=rwkv_r_k.reshape(N_RWKV, 1, D), e=e, et=e.T,
    )
    n1 = vec(norm1_g)
    n2 = vec(norm2_g)
    w_pair = bf(sgu_w_s.reshape(-1, N_PAIRS, 2, CHUNK, CHUNK).transpose(0, 1, 3, 2, 4)
                .reshape(-1, N_PAIRS, CHUNK, 2 * CHUNK))
    bmap = jnp.repeat(jnp.swapaxes(sgu_b_s, 1, 2), D // SGU_GROUPS, axis=2)
    sgu_w_in_b, sgu_w_out_b = bf(sgu_w_in), bf(sgu_w_out)
    mlp_w1_b, mlp_w2_b = bf(mlp_w1), bf(mlp_w2)
    w_o_b = bf(rwkv_w_o)
    k_k, ln_g, ln_b = vec(rwkv_k_k), vec(rwkv_ln_g), vec(rwkv_ln_b)
    sgu_g, sgu_b = vec(sgu_ln_g), vec(sgu_ln_b)

    s_lat = state_wkv.transpose(1, 0, 2, 4, 3, 5).reshape(N_RWKV, DEC_BATCH, 2, HEAD_DIM, D)
    s_init = jnp.concatenate([jnp.zeros((N_RWKV, BATCH, 2, HEAD_DIM, D), F32), s_lat], axis=1)

    xs = (x_prompt.reshape(N_CTX, D), x_sample.reshape(N_LAT, D))
    v_first = None
    finals = []
    for l in range(DEPTH):
        i = l // 2
        if l % 2 == 0:
            x = _sgu_call(xs, mod, l, n1, sgu_w_in_b, sgu_g, sgu_b, w_pair, bmap, sgu_w_out_b)
        else:
            r, k, v, g, bv, lw, a = _proj_call(x, mod, l, n1, W, v_first)
            if v_first is None:
                v_first = v
            yf, yb, s_fin = _wkv_call(r, k, v, lw, a, k_k, W['k_a'], s_init[i], i)
            finals.append(s_fin[:BATCH])
            x = _post_call(x, mod, l, yf, yb, bv, g, ln_g, ln_b, W['e'], W['et'], w_o_b)
        x = _mlp_call(x, mod, l, n2, mlp_w1_b, mlp_w2_b)
        xs = (x,)

    y_prompt = _final_call(x, final_g.reshape(1, D), 0, N_CTX).reshape(BATCH, SEQ, D)
    y_sample = _final_call(x, final_g.reshape(1, D), N_CTX, N_LAT).reshape(DEC_BATCH, DEC_SEQ, D)
    new_state = jnp.stack(finals, axis=1).reshape(BATCH, N_RWKV, 2, HEAD_DIM, N_HEADS, HEAD_DIM)
    new_state = new_state.transpose(0, 1, 2, 4, 3, 5)
    return (y_prompt, y_sample, new_state)
```

```python
import functools

import numpy as np
import jax
import jax.numpy as jnp
from jax import lax
from jax.experimental import pallas as pl
from jax.experimental.pallas import tpu as pltpu

D_MODEL = 1024
BATCH = 16
SEQ = 256
DEPTH = 4
DEC_BATCH = 8
DEC_SEQ = 2048
N_RWKV = DEPTH // 2
CHUNK = 128
SGU_GROUPS = 16
HEAD_DIM = 64
N_HEADS = D_MODEL // HEAD_DIM
LORA_V = 32
D_FF = 4 * D_MODEL
NORM_EPS = 1e-6
GN_EPS = 64e-5

LANES = 128
N_PAIRS = D_MODEL // LANES
N_CTX = BATCH * SEQ
N_LAT = DEC_BATCH * DEC_SEQ
N_TOK = N_CTX + N_LAT
N_SEQ = BATCH + DEC_BATCH
COND_ROWS = 16

WKV_T = 64
WKV_NS = 2
NC_CTX = SEQ // WKV_T
NC_LAT = DEC_SEQ // WKV_T
WKV_STEPS_CTX = (BATCH // WKV_NS) * NC_CTX
WKV_STEPS = WKV_STEPS_CTX + (DEC_BATCH // WKV_NS) * NC_LAT

N_GROUPED = N_TOK // (WKV_NS * WKV_T)
TM_RWKV = 256
CHUNKS_PER_BLOCK = TM_RWKV // WKV_T

TM_SGU = 512
TM_MLP = 512
TM_NORM = 512
FF_CHUNK = 1024
ADA_TN = 1536
VMEM_LIMIT = 56 * 1024 * 1024

F32 = jnp.float32
BF16 = jnp.bfloat16


def _dot(a, b):
    return jnp.dot(a, b, preferred_element_type=F32)


def _dot_nt(a, b):
    return lax.dot_general(a, b, (((1,), (1,)), ((), ())), preferred_element_type=F32)


def _dot_tn(a, b):
    return lax.dot_general(a, b, (((0,), (0,)), ((), ())), preferred_element_type=F32)


def _split2(x):
    h1 = x.astype(BF16)
    return h1, (x - h1.astype(F32)).astype(BF16)


def _dot_exact_rhs(x, m):
    h1, h2 = _split2(x)
    return _dot(h1, m) + _dot(h2, m)


def _dot_exact_lhs(m, x):
    h1, h2 = _split2(x)
    return _dot(m, h1) + _dot(m, h2)


def _sigmoid(x):
    return 1.0 / (1.0 + jnp.exp(-x))


def _norm_mod(x, g, scale, shift):
    ms = jnp.mean(x * x, axis=-1, keepdims=True)
    return (x * lax.rsqrt(ms + NORM_EPS) * g) * (1.0 + scale) + shift


def _stack_heads(x, lo):
    return jnp.concatenate([jnp.where(lo, x, 0.0), jnp.where(lo, 0.0, x)], axis=0)


def _cond_row(i, tm):
    n_ctx_blocks = N_CTX // tm
    return jnp.where(i < n_ctx_blocks, 0, 1 + (i - n_ctx_blocks) // (DEC_SEQ // tm))


def _const_spec(shape):
    nd = len(shape)
    return pl.BlockSpec(shape, lambda *_: (0,) * nd, pipeline_mode=pl.Buffered(1))


def _layer_spec(shape, l):
    nd = len(shape)
    return pl.BlockSpec((None,) + tuple(shape), lambda *_: (l,) + (0,) * nd, pipeline_mode=pl.Buffered(1))


def _mod_spec(tm, l):
    return pl.BlockSpec((None, None, 1, 6 * D_MODEL), lambda i: (l, _cond_row(i, tm), 0, 0))


def _grouped_index(i):
    n_ctx_blocks = N_CTX // TM_RWKV
    per_seq = DEC_SEQ // TM_RWKV
    in_ctx = i < n_ctx_blocks
    il = jnp.maximum(i - n_ctx_blocks, 0)
    seq_l = il // per_seq
    blk = jnp.where(in_ctx, i // WKV_NS, n_ctx_blocks // WKV_NS + (seq_l // WKV_NS) * per_seq + il % per_seq)
    slot = jnp.where(in_ctx, i % WKV_NS, seq_l % WKV_NS)
    return blk, slot


def _grouped_spec():
    return pl.BlockSpec((CHUNKS_PER_BLOCK, None, WKV_T, D_MODEL), lambda i: (*_grouped_index(i), 0, 0))


def _grouped_spec2():
    return pl.BlockSpec((2, CHUNKS_PER_BLOCK, None, WKV_T, D_MODEL), lambda i: (0, *_grouped_index(i), 0, 0))


def _params(n_axes=1):
    return pltpu.CompilerParams(
        dimension_semantics=("arbitrary",) * n_axes, vmem_limit_bytes=VMEM_LIMIT)


def _ada_kernel(cond_ref, w_ref, b_ref, o_ref):
    c = cond_ref[...]
    s = c * _sigmoid(c)
    o_ref[...] = _dot(s.astype(BF16), w_ref[...].astype(BF16)) + b_ref[...]


def _ada_call(cond, ada_w, ada_b):
    return pl.pallas_call(
        _ada_kernel,
        grid=(DEPTH, 6 * D_MODEL // ADA_TN),
        in_specs=[
            pl.BlockSpec((COND_ROWS, D_MODEL), lambda l, n: (0, 0)),
            pl.BlockSpec((None, D_MODEL, ADA_TN), lambda l, n: (l, 0, n)),
            pl.BlockSpec((None, 1, ADA_TN), lambda l, n: (l, 0, n)),
        ],
        out_specs=pl.BlockSpec((None, COND_ROWS, ADA_TN), lambda l, n: (l, 0, n)),
        out_shape=jax.ShapeDtypeStruct((DEPTH, COND_ROWS, 6 * D_MODEL), F32),
        compiler_params=_params(2),
        name="ada_mod",
    )(cond, ada_w, ada_b.reshape(DEPTH, 1, 6 * D_MODEL))


def _sgu_kernel(split_input, *refs):
    if split_input:
        xa_ref, xb_ref, mod_ref, ng_ref, win_ref, lng_ref, lnb_ref, wp_ref, bmap_ref, wout_ref, o_ref = refs
        x = jnp.where(pl.program_id(0) < N_CTX // TM_SGU, xa_ref[...], xb_ref[...])
    else:
        x_ref, mod_ref, ng_ref, win_ref, lng_ref, lnb_ref, wp_ref, bmap_ref, wout_ref, o_ref = refs
        x = x_ref[...]
    mod = mod_ref[...]
    shift, scale, gate = (mod[:, j * D_MODEL:(j + 1) * D_MODEL] for j in range(3))
    h = _norm_mod(x, ng_ref[...], scale, shift)
    z = _dot(h.astype(BF16), win_ref[...])
    z = z * (0.5 * (1.0 + jnp.tanh(0.7978845608028654 * (z + 0.044715 * (z * z * z)))))
    u = z[:, :D_MODEL]
    v = z[:, D_MODEL:]
    mu = jnp.mean(v, axis=-1, keepdims=True)
    dv = v - mu
    var = jnp.mean(dv * dv, axis=-1, keepdims=True)
    v = dv * lax.rsqrt(var + NORM_EPS) * lng_ref[...] + lnb_ref[...]
    lo = lax.broadcasted_iota(jnp.int32, (CHUNK, LANES), 1) < HEAD_DIM
    bmap = bmap_ref[...]
    n_chunks = TM_SGU // CHUNK
    cols = []
    for p in range(N_PAIRS):
        rhs = jnp.concatenate(
            [_stack_heads(v[c * CHUNK:(c + 1) * CHUNK, p * LANES:(p + 1) * LANES], lo) for c in range(n_chunks)],
            axis=1).astype(BF16)
        mixed = _dot(wp_ref[p], rhs)
        cols.append(jnp.concatenate([mixed[:, c * LANES:(c + 1) * LANES] for c in range(n_chunks)], axis=0))
    vs = jnp.concatenate(cols, axis=1) + jnp.concatenate([bmap] * n_chunks, axis=0)
    m = _dot((u * vs).astype(BF16), wout_ref[...])
    o_ref[...] = x + gate * m


def _sgu_call(xs, mod, l, ng, w_in, ln_g, ln_b, w_pair, bmap, w_out):
    i = l // 2
    tm = TM_SGU
    row = pl.BlockSpec((tm, D_MODEL), lambda b: (b, 0))
    if len(xs) == 2:
        n_ctx_blocks = N_CTX // tm
        x_specs = [pl.BlockSpec((tm, D_MODEL), lambda b: (jnp.minimum(b, n_ctx_blocks - 1), 0)),
                   pl.BlockSpec((tm, D_MODEL), lambda b: (jnp.maximum(b - n_ctx_blocks, 0), 0))]
    else:
        x_specs = [row]
    return pl.pallas_call(
        functools.partial(_sgu_kernel, len(xs) == 2),
        grid=(N_TOK // tm,),
        in_specs=x_specs + [
            _mod_spec(tm, l),
            _layer_spec((1, D_MODEL), l),
            _layer_spec((D_MODEL, 2 * D_MODEL), i),
            _layer_spec((1, D_MODEL), i), _layer_spec((1, D_MODEL), i),
            _layer_spec((N_PAIRS, CHUNK, 2 * CHUNK), i),
            _layer_spec((CHUNK, D_MODEL), i),
            _layer_spec((D_MODEL, D_MODEL), i),
        ],
        out_specs=row,
        out_shape=jax.ShapeDtypeStruct((N_TOK, D_MODEL), F32),
        compiler_params=_params(),
        name="sgu_layer",
    )(*xs, mod, ng, w_in, ln_g, ln_b, w_pair, bmap, w_out)


def _mlp_kernel(x_ref, mod_ref, ng_ref, w1_ref, w2_ref, o_ref):
    x = x_ref[...]
    mod = mod_ref[...]
    shift, scale, gate = (mod[:, j * D_MODEL:(j + 1) * D_MODEL] for j in range(3, 6))
    h = _norm_mod(x, ng_ref[...], scale, shift).astype(BF16)
    acc = jnp.zeros((TM_MLP, D_MODEL), F32)
    for c in range(D_FF // FF_CHUNK):
        a = _dot(h, w1_ref[:, c * FF_CHUNK:(c + 1) * FF_CHUNK])
        a = jnp.square(jnp.maximum(a, 0.0))
        acc = acc + _dot(a.astype(BF16), w2_ref[c * FF_CHUNK:(c + 1) * FF_CHUNK, :])
    o_ref[...] = x + gate * acc


def _mlp_call(x, mod, l, ng, w1, w2):
    row = pl.BlockSpec((TM_MLP, D_MODEL), lambda i: (i, 0))
    return pl.pallas_call(
        _mlp_kernel,
        grid=(N_TOK // TM_MLP,),
        in_specs=[row, _mod_spec(TM_MLP, l), _layer_spec((1, D_MODEL), l),
                  _layer_spec((D_MODEL, D_FF), l), _layer_spec((D_FF, D_MODEL), l)],
        out_specs=row,
        out_shape=jax.ShapeDtypeStruct((N_TOK, D_MODEL), F32),
        compiler_params=_params(),
        name="mlp_layer",
    )(x, mod, ng, w1, w2)


def _proj_kernel(has_vmix, *refs):
    if has_vmix:
        (x_ref, xp_ref, xn_ref, mod_ref, ng_ref, mu_ref, wr_ref, wk_ref, wv_ref, g1_ref, g2_ref,
         w0_ref, w1_ref, w2_ref, a0_ref, a1_ref, a2_ref, ka_ref, rk_ref, e_ref, et_ref,
         vf_ref, v0_ref, v1_ref, v2_ref,
         r_out, k_out, v_out, g_out, bv_out, lw_out, a_out) = refs
    else:
        (x_ref, xp_ref, xn_ref, mod_ref, ng_ref, mu_ref, wr_ref, wk_ref, wv_ref, g1_ref, g2_ref,
         w0_ref, w1_ref, w2_ref, a0_ref, a1_ref, a2_ref, ka_ref, rk_ref, e_ref, et_ref,
         r_out, k_out, v_out, g_out, bv_out, lw_out, a_out) = refs
    tm = TM_RWKV
    grouped = (CHUNKS_PER_BLOCK, WKV_T, D_MODEL)
    i = pl.program_id(0)
    t0 = i * tm
    in_ctx = t0 < N_CTX
    pos = jnp.where(in_ctx, t0 & (SEQ - 1), (t0 - N_CTX) & (DEC_SEQ - 1))
    seq_len = jnp.where(in_ctx, SEQ, DEC_SEQ)
    has_prev = jnp.where(pos != 0, 1.0, 0.0)
    has_next = jnp.where(pos + tm != seq_len, 1.0, 0.0)

    mod = mod_ref[...]
    shift, scale = mod[:, :D_MODEL], mod[:, D_MODEL:2 * D_MODEL]
    ng = ng_ref[...]
    h = _norm_mod(x_ref[...], ng, scale, shift)
    h_prev = _norm_mod(xp_ref[7:8, :], ng, scale, shift) * has_prev
    h_next = _norm_mod(xn_ref[0:1, :], ng, scale, shift) * has_next
    rows = lax.broadcasted_iota(jnp.int32, (tm, D_MODEL), 0)
    h_dn = jnp.where(rows == 0, h_prev, pltpu.roll(h, 1, 0))
    h_up = jnp.where(rows == tm - 1, h_next, pltpu.roll(h, tm - 1, 0))
    xx = 0.5 * (h_dn + h_up) - h
    mu = mu_ref[...]
    xr, xw, xk, xv, xa, xg = ((h + xx * mu[j:j + 1, :]).astype(BF16) for j in range(6))

    r = _dot(xr, wr_ref[...])
    k = _dot(xk, wk_ref[...])
    v = _dot(xv, wv_ref[...])
    if has_vmix:
        mix = _sigmoid(v0_ref[...] + _dot(_dot(xv, v1_ref[...]).astype(BF16), v2_ref[...]))
        v = v + (vf_ref[...].reshape(tm, D_MODEL) - v) * mix
    g = _dot(_sigmoid(_dot(xg, g1_ref[...])).astype(BF16), g2_ref[...])

    lo = lax.broadcasted_iota(jnp.int32, (tm, LANES), 1) < HEAD_DIM
    th = jnp.tanh(_dot(xw, w1_ref[...]))
    al = _dot(xa, a1_ref[...])
    ka = ka_ref[...]
    ksum = jnp.zeros_like(k)
    for d in range(2):
        keep = lo if d == 0 else jnp.logical_not(lo)
        z = w0_ref[d:d + 1, :] + _dot(jnp.where(keep, th, 0.0).astype(BF16), w2_ref[...])
        lw_out[d] = (-0.6065306597126334 * _sigmoid(z)).reshape(grouped)
        a_d = _sigmoid(a0_ref[d:d + 1, :] + _dot(jnp.where(keep, al, 0.0).astype(BF16), a2_ref[...]))
        a_out[d] = a_d.reshape(grouped)
        ksum = ksum + k * (1.0 + (a_d - 1.0) * ka)
    head_sum = _dot_exact_rhs(r * ksum * rk_ref[...], e_ref[...])
    bonus = _dot_exact_rhs(head_sum, et_ref[...])
    r_out[...] = r.reshape(grouped)
    k_out[...] = k.reshape(grouped)
    v_out[...] = v.reshape(grouped)
    g_out[...] = g.reshape(grouped)
    bv_out[...] = (bonus * v).reshape(grouped)


def _proj_call(x, mod, l, ng, W, v_first):
    i = l // 2
    tm = TM_RWKV
    nb8 = tm // 8
    has_vmix = v_first is not None
    row = pl.BlockSpec((tm, D_MODEL), lambda b: (b, 0))
    prev = pl.BlockSpec((8, D_MODEL), lambda b: (jnp.maximum(b * nb8 - 1, 0), 0))
    nxt = pl.BlockSpec((8, D_MODEL), lambda b: (jnp.minimum((b + 1) * nb8, N_TOK // 8 - 1), 0))
    mat = _layer_spec((D_MODEL, D_MODEL), i)
    vec = _layer_spec((1, D_MODEL), i)
    vec2 = _layer_spec((2, D_MODEL), i)
    down = _layer_spec((D_MODEL, LANES), i)
    up = _layer_spec((LANES, D_MODEL), i)
    in_specs = [row, prev, nxt, _mod_spec(tm, l), _layer_spec((1, D_MODEL), l), _layer_spec((6, D_MODEL), i),
                mat, mat, mat, down, up,
                vec2, down, up, vec2, down, up, vec, vec,
                _const_spec((D_MODEL, LANES)), _const_spec((LANES, D_MODEL))]
    args = [x, x, x, mod, ng, W['mu'], W['w_r'], W['w_k'], W['w_v'], W['g1'], W['g2'],
            W['w0'], W['w1'], W['w2'], W['a0'], W['a1'], W['a2'], W['k_a'], W['r_k'], W['e'], W['et']]
    if has_vmix:
        in_specs += [_grouped_spec(), _layer_spec((1, D_MODEL), i - 1), _layer_spec((D_MODEL, LANES), i - 1),
                     _layer_spec((LANES, D_MODEL), i - 1)]
        args += [v_first, W['v0'], W['v1'], W['v2']]
    tok = jax.ShapeDtypeStruct((N_GROUPED, WKV_NS, WKV_T, D_MODEL), F32)
    tok2 = jax.ShapeDtypeStruct((2, N_GROUPED, WKV_NS, WKV_T, D_MODEL), F32)
    return pl.pallas_call(
        functools.partial(_proj_kernel, has_vmix),
        grid=(N_TOK // tm,),
        in_specs=in_specs,
        out_specs=[_grouped_spec()] * 5 + [_grouped_spec2()] * 2,
        out_shape=[tok] * 5 + [tok2] * 2,
        compiler_params=_params(),
        name="rwkv_proj",
    )(*args)


def _wkv_decode(i):
    in_ctx = i < WKV_STEPS_CTX
    il = jnp.maximum(i - WKV_STEPS_CTX, 0)
    grp = jnp.where(in_ctx, i // NC_CTX, BATCH // WKV_NS + il // NC_LAT)
    j = jnp.where(in_ctx, i % NC_CTX, il % NC_LAT)
    nc = jnp.where(in_ctx, NC_CTX, NC_LAT)
    base = jnp.where(in_ctx, (i // NC_CTX) * NC_CTX, WKV_STEPS_CTX + (il // NC_LAT) * NC_LAT)
    return grp, j, nc, base


def _wkv_entry(i, d):
    _, j, nc, base = _wkv_decode(i)
    return base + (j if d == 0 else nc - 1 - j)


def _wkv_masks():
    T = WKV_T
    t = np.arange(T)[:, None]
    s1 = np.arange(T)[None, :]
    s2 = np.arange(2 * T)[None, :] % T
    s4 = np.arange(4 * T)[None, :] % T
    cum, strict, incl, lvl = [], [], [], []
    for d in range(2):
        early = (lambda s: s < t) if d == 0 else (lambda s: s > t)
        cum.append(early(s1) | (s1 == t))
        strict.append(early(s2))
        incl.append(early(s4) | (s4 == t))
        lvl.append([early(s2) & (((t ^ s2) >> lb) == 1) for lb in range(6)])
    hd = np.arange(2 * T) // HEAD_DIM
    return (jnp.asarray(np.array(cum), BF16), jnp.asarray(np.array(strict), F32),
            jnp.asarray(np.array(incl), F32), jnp.asarray(np.array(lvl), F32),
            jnp.asarray(hd[:, None] == hd[None, :], F32))


def _wkv_kernel(rf_ref, kf_ref, vf_ref, lwf_ref, af_ref, rb_ref, kb_ref, vb_ref, lwb_ref, ab_ref,
                kkw_ref, ka_ref, cum_ref, strict_ref, incl_ref, lvl_ref, same_ref, s0_ref,
                yf_ref, yb_ref, sf_ref, s_scr):
    T = WKV_T
    _, j, nc, _ = _wkv_decode(pl.program_id(0))
    lo = lax.broadcasted_iota(jnp.int32, (T, LANES), 1) < HEAD_DIM
    pairs = range(N_PAIRS)
    sls = [slice(p * LANES, (p + 1) * LANES) for p in pairs]
    chains = [(q, d) for q in range(WKV_NS) for d in range(2)]
    units = [(q, d, p) for q, d in chains for p in pairs]

    @pl.when(j == 0)
    def _():
        for q, d, p in units:
            s_scr[q, d, p] = _stack_heads(s0_ref[q, d, :, sls[p]], lo)

    def stk(x):
        return _stack_heads(x.astype(BF16), lo)

    kkw = kkw_ref[...]
    ka = ka_ref[...]
    dir_refs = ((rf_ref, kf_ref, vf_ref, lwf_ref, af_ref), (rb_ref, kb_ref, vb_ref, lwb_ref, ab_ref))
    prep_pieces = 2 * (1 + N_PAIRS // 2)

    def scan(q):
        units = [(d, p) for d in range(2) for p in pairs]
        ar, bk, bkc, v2, vv, c_tot = {}, {}, {}, {}, {}, {}
        for d in range(2):
            r_ref, k_ref, v_ref, lw_ref, a_ref = dir_refs[d]
            lw = lw_ref[q]
            logc = _dot_exact_lhs(cum_ref[d], lw)
            tot = jnp.sum(lw, axis=0, keepdims=True)
            c = jnp.exp(logc)
            c_inv = jnp.exp(-logc)
            c_prev = jnp.exp(logc - lw)
            c_tot[d] = jnp.exp(tot)
            c_end = c_tot[d] * c_inv
            r = r_ref[q]
            k = k_ref[q]
            a = a_ref[q]
            vv[d] = v_ref[q]
            kk_raw = k * kkw
            kd = k * (1.0 + (a - 1.0) * ka)
            yield
            for p in pairs:
                sl = sls[p]
                kq = kk_raw[:, sl]
                sq = kq * kq
                n_lo = jnp.sum(jnp.where(lo, sq, 0.0), axis=-1, keepdims=True)
                n_hi = jnp.sum(jnp.where(lo, 0.0, sq), axis=-1, keepdims=True)
                kk = kq * lax.rsqrt(jnp.maximum(jnp.where(lo, n_lo, n_hi), 1e-24))
                beta = kk * a[:, sl]
                u = (d, p)
                ar[u] = jnp.concatenate([-kk * c_prev[:, sl], r[:, sl] * c[:, sl]], axis=0).astype(BF16)
                bk[u] = jnp.concatenate([stk(beta * c_inv[:, sl]), stk(kd[:, sl] * c_inv[:, sl])], axis=0)
                bkc[u] = jnp.concatenate([beta * c_end[:, sl], kd[:, sl] * c_end[:, sl]], axis=0).astype(BF16)
                v2[u] = stk(vv[d][:, sl])
                if p % 2 == 1:
                    yield
        gm = {u: _dot_nt(ar[u], bk[u]) for u in units}
        yield
        uy0 = {(d, p): _dot_nt(ar[d, p], s_scr[q, d, p].astype(BF16)) for d, p in units}
        yield
        kv_mask = {d: jnp.concatenate([strict_ref[d], incl_ref[d][:, 2 * T:]], axis=0) for d in range(2)}
        xv = {u: _dot((gm[u][:, 2 * T:] * kv_mask[u[0]]).astype(BF16), v2[u]) for u in units}
        x = {u: uy0[u][:T] + xv[u][:T] for u in units}
        yield
        w = {u: gm[u][:T, :2 * T] * lvl_ref[u[0], 0] for u in units}
        for lb in range(1, 6):
            p_off = {u: gm[u][:T, :2 * T] * lvl_ref[u[0], lb] for u in units}
            t1 = {u: p_off[u] + _dot(p_off[u].astype(BF16), stk(w[u])) for u in units}
            yield
            w = {u: w[u] + t1[u] + _dot(w[u].astype(BF16), stk(t1[u])) for u in units}
            yield
        uu = {u: x[u] + _dot(w[u].astype(BF16), stk(x[u])) for u in units}
        yield
        for d, p in units:
            y_ref = yf_ref if d == 0 else yb_ref
            rb = (gm[d, p][T:, :2 * T] * incl_ref[d][:, :2 * T]).astype(BF16)
            y_ref[q, :, sls[p]] = uy0[d, p][T:] + xv[d, p][T:] + _dot(rb, stk(uu[d, p]))
        yield
        for d, p in units:
            uv = jnp.concatenate([uu[d, p], vv[d][:, sls[p]]], axis=0).astype(BF16)
            s_scr[q, d, p] = (s_scr[q, d, p] * c_tot[d][:, sls[p]]
                              + same_ref[...] * _dot_tn(uv, bkc[d, p]))

    scans = [scan(q) for q in range(WKV_NS)]
    live = [True] * WKV_NS
    tick = 0
    while any(live):
        for q in range(WKV_NS):
            if live[q] and tick >= q * prep_pieces:
                live[q] = next(scans[q], "done") != "done"
        tick += 1

    @pl.when(j == nc - 1)
    def _():
        for q, d, p in units:
            sf_ref[q, d, :, sls[p]] = s_scr[q, d, p, :HEAD_DIM, :] + s_scr[q, d, p, HEAD_DIM:, :]


def _wkv_call(r, k, v, lw, a, kkw, ka, s0, i):
    blk = (WKV_NS, WKV_T, D_MODEL)
    tok = [pl.BlockSpec((None,) + blk, functools.partial(lambda d, b: (_wkv_entry(b, d), 0, 0, 0), d))
           for d in range(2)]
    tok_d = [pl.BlockSpec((None, None) + blk, functools.partial(lambda d, b: (d, _wkv_entry(b, d), 0, 0, 0), d))
             for d in range(2)]
    state = pl.BlockSpec((WKV_NS, 2, HEAD_DIM, D_MODEL), lambda b: (_wkv_decode(b)[0], 0, 0, 0))
    vec = _layer_spec((1, D_MODEL), i)
    masks = _wkv_masks()
    y_shape = jax.ShapeDtypeStruct((N_GROUPED,) + blk, F32)
    return pl.pallas_call(
        _wkv_kernel,
        grid=(WKV_STEPS,),
        in_specs=[tok[0], tok[0], tok[0], tok_d[0], tok_d[0], tok[1], tok[1], tok[1], tok_d[1], tok_d[1],
                  vec, vec] + [_const_spec(m.shape) for m in masks] + [state],
        out_specs=[tok[0], tok[1], state],
        out_shape=[y_shape, y_shape, jax.ShapeDtypeStruct((N_SEQ, 2, HEAD_DIM, D_MODEL), F32)],
        scratch_shapes=[pltpu.VMEM((WKV_NS, 2, N_PAIRS, 2 * HEAD_DIM, LANES), F32)],
        compiler_params=_params(),
        name="wkv_scan",
    )(r, k, v, lw, a, r, k, v, lw, a, kkw, ka, *masks, s0)


def _post_kernel(x_ref, mod_ref, yf_ref, yb_ref, bv_ref, g_ref, lng_ref, lnb_ref, e_ref, et_ref, wo_ref,
                 o_ref):
    rows = (TM_RWKV, D_MODEL)
    gate = mod_ref[...][:, 2 * D_MODEL:3 * D_MODEL]
    y = (yf_ref[...] + yb_ref[...]).reshape(rows)
    e = e_ref[...]
    et = et_ref[...]
    inv_n = 1.0 / HEAD_DIM
    mu = _dot_exact_rhs(_dot_exact_rhs(y, e), et) * inv_n
    dv = y - mu
    var = _dot_exact_rhs(_dot_exact_rhs(dv * dv, e), et) * inv_n
    yn = dv * lax.rsqrt(var + GN_EPS) * lng_ref[...] + lnb_ref[...]
    yy = (yn + bv_ref[...].reshape(rows)) * g_ref[...].reshape(rows)
    o_ref[...] = x_ref[...] + gate * _dot(yy.astype(BF16), wo_ref[...])


def _post_call(x, mod, l, yf, yb, bv, g, ln_g, ln_b, e, et, w_o):
    i = l // 2
    tm = TM_RWKV
    row = pl.BlockSpec((tm, D_MODEL), lambda b: (b, 0))
    grouped = _grouped_spec()
    vec = _layer_spec((1, D_MODEL), i)
    return pl.pallas_call(
        _post_kernel,
        grid=(N_TOK // tm,),
        in_specs=[row, _mod_spec(tm, l), grouped, grouped, grouped, grouped, vec, vec,
                  _const_spec((D_MODEL, LANES)), _const_spec((LANES, D_MODEL)),
                  _layer_spec((D_MODEL, D_MODEL), i)],
        out_specs=row,
        out_shape=jax.ShapeDtypeStruct((N_TOK, D_MODEL), F32),
        compiler_params=_params(),
        name="rwkv_out",
    )(x, mod, yf, yb, bv, g, ln_g, ln_b, e, et, w_o)


def _final_kernel(x_ref, g_ref, o_ref):
    x = x_ref[...]
    ms = jnp.mean(x * x, axis=-1, keepdims=True)
    o_ref[...] = x * lax.rsqrt(ms + NORM_EPS) * g_ref[...]


def _final_call(x, g, row0, n_rows):
    tm = TM_NORM
    b0 = row0 // tm
    return pl.pallas_call(
        _final_kernel,
        grid=(n_rows // tm,),
        in_specs=[pl.BlockSpec((tm, D_MODEL), lambda i: (i + b0, 0)), _const_spec((1, D_MODEL))],
        out_specs=pl.BlockSpec((tm, D_MODEL), lambda i: (i, 0)),
        out_shape=jax.ShapeDtypeStruct((n_rows, D_MODEL), F32),
        compiler_params=_params(),
        name="final_norm",
    )(x, g)


def _pad_cols(w, n):
    return jnp.pad(w, [(0, 0)] * (w.ndim - 1) + [(0, n - w.shape[-1])])


def _pad_rows(w, n):
    return jnp.pad(w, [(0, 0)] * (w.ndim - 2) + [(0, n - w.shape[-2]), (0, 0)])


def kernel(x_prompt, x_sample, state_wkv, c, c_ctx, norm1_g, norm2_g, ada_w, ada_b, sgu_w_in, sgu_ln_g,
           sgu_ln_b, sgu_w_s, sgu_b_s, sgu_w_out, rwkv_mu, rwkv_w_r, rwkv_w_k, rwkv_w_v, rwkv_w_o,
           rwkv_w0, rwkv_w1, rwkv_w2, rwkv_a0, rwkv_a1, rwkv_a2, rwkv_v0, rwkv_v1, rwkv_v2, rwkv_g1,
           rwkv_g2, rwkv_k_k, rwkv_k_a, rwkv_r_k, rwkv_ln_g, rwkv_ln_b, mlp_w1, mlp_w2, final_g):
    D = D_MODEL
    bf = lambda w: w.astype(BF16)
    vec = lambda w: w.reshape(w.shape[0], 1, D)

    cond = jnp.concatenate([c_ctx[None, :], c, jnp.zeros((COND_ROWS - 1 - DEC_BATCH, D), F32)], axis=0)
    mod = _ada_call(cond, ada_w, ada_b).reshape(DEPTH, COND_ROWS, 1, 6 * D)

    head_of = jnp.arange(D) // HEAD_DIM
    e = (head_of[:, None] == jnp.arange(LANES)[None, :]).astype(BF16)
    W = dict(
        mu=rwkv_mu, w_r=bf(rwkv_w_r), w_k=bf(rwkv_w_k), w_v=bf(rwkv_w_v),
        g1=bf(rwkv_g1), g2=bf(rwkv_g2),
        w0=rwkv_w0,
        w1=bf(jnp.concatenate([rwkv_w1[:, 0], rwkv_w1[:, 1]], axis=-1)),
        w2=bf(jnp.concatenate([rwkv_w2[:, 0], rwkv_w2[:, 1]], axis=-2)),
        a0=rwkv_a0,
        a1=bf(jnp.concatenate([rwkv_a1[:, 0], rwkv_a1[:, 1]], axis=-1)),
        a2=bf(jnp.concatenate([rwkv_a2[:, 0], rwkv_a2[:, 1]], axis=-2)),
        v0=vec(rwkv_v0), v1=bf(_pad_cols(rwkv_v1, LANES)), v2=bf(_pad_rows(rwkv_v2, LANES)),
        k_a=vec(rwkv_k_a), r_k=rwkv_r_k.reshape(N_RWKV, 1, D), e=e, et=e.T,
    )
    n1 = vec(norm1_g)
    n2 = vec(norm2_g)
    w_pair = bf(sgu_w_s.reshape(-1, N_PAIRS, 2, CHUNK, CHUNK).transpose(0, 1, 3, 2, 4)
                .reshape(-1, N_PAIRS, CHUNK, 2 * CHUNK))
    bmap = jnp.repeat(jnp.swapaxes(sgu_b_s, 1, 2), D // SGU_GROUPS, axis=2)
    sgu_w_in_b, sgu_w_out_b = bf(sgu_w_in), bf(sgu_w_out)
    mlp_w1_b, mlp_w2_b = bf(mlp_w1), bf(mlp_w2)
    w_o_b = bf(rwkv_w_o)
    k_k, ln_g, ln_b = vec(rwkv_k_k), vec(rwkv_ln_g), vec(rwkv_ln_b)
    sgu_g, sgu_b = vec(sgu_ln_g), vec(sgu_ln_b)

    s_lat = state_wkv.transpose(1, 0, 2, 4, 3, 5).reshape(N_RWKV, DEC_BATCH, 2, HEAD_DIM, D)
    s_init = jnp.concatenate([jnp.zeros((N_RWKV, BATCH, 2, HEAD_DIM, D), F32), s_lat], axis=1)

    xs = (x_prompt.reshape(N_CTX, D), x_sample.reshape(N_LAT, D))
    v_first = None
    finals = []
    for l in range(DEPTH):
        i = l // 2
        if l % 2 == 0:
            x = _sgu_call(xs, mod, l, n1, sgu_w_in_b, sgu_g, sgu_b, w_pair, bmap, sgu_w_out_b)
        else:
            r, k, v, g, bv, lw, a = _proj_call(x, mod, l, n1, W, v_first)
            if v_first is None:
                v_first = v
            yf, yb, s_fin = _wkv_call(r, k, v, lw, a, k_k, W['k_a'], s_init[i], i)
            finals.append(s_fin[:BATCH])
            x = _post_call(x, mod, l, yf, yb, bv, g, ln_g, ln_b, W['e'], W['et'], w_o_b)
        x = _mlp_call(x, mod, l, n2, mlp_w1_b, mlp_w2_b)
        xs = (x,)

    y_prompt = _final_call(x, final_g.reshape(1, D), 0, N_CTX).reshape(BATCH, SEQ, D)
    y_sample = _final_call(x, final_g.reshape(1, D), N_CTX, N_LAT).reshape(DEC_BATCH, DEC_SEQ, D)
    new_state = jnp.stack(finals, axis=1).reshape(BATCH, N_RWKV, 2, HEAD_DIM, N_HEADS, HEAD_DIM)
    new_state = new_state.transpose(0, 1, 2, 4, 3, 5)
    return (y_prompt, y_sample, new_state)
```

```python
import functools

import numpy as np
import jax
import jax.numpy as jnp
from jax import lax
from jax.experimental import pallas as pl
from jax.experimental.pallas import tpu as pltpu

D_MODEL = 1024
BATCH = 16
SEQ = 256
DEPTH = 4
DEC_BATCH = 8
DEC_SEQ = 2048
N_RWKV = DEPTH // 2
CHUNK = 128
SGU_GROUPS = 16
HEAD_DIM = 64
N_HEADS = D_MODEL // HEAD_DIM
LORA_V = 32
D_FF = 4 * D_MODEL
NORM_EPS = 1e-6
GN_EPS = 64e-5

LANES = 128
N_PAIRS = D_MODEL // LANES
N_CTX = BATCH * SEQ
N_LAT = DEC_BATCH * DEC_SEQ
N_TOK = N_CTX + N_LAT
N_SEQ = BATCH + DEC_BATCH
COND_ROWS = 16

WKV_T = 64
WKV_NS = 2
NC_CTX = SEQ // WKV_T
NC_LAT = DEC_SEQ // WKV_T
WKV_STEPS_CTX = (BATCH // WKV_NS) * NC_CTX
WKV_STEPS = WKV_STEPS_CTX + (DEC_BATCH // WKV_NS) * NC_LAT

N_GROUPED = N_TOK // (WKV_NS * WKV_T)
TM_RWKV = 256
CHUNKS_PER_BLOCK = TM_RWKV // WKV_T

TM_SGU = 512
TM_MLP = 512
TM_NORM = 512
FF_CHUNK = 1024
ADA_TN = 1536
VMEM_LIMIT = 56 * 1024 * 1024

F32 = jnp.float32
BF16 = jnp.bfloat16


def _dot(a, b):
    return jnp.dot(a, b, preferred_element_type=F32)


def _dot_nt(a, b):
    return lax.dot_general(a, b, (((1,), (1,)), ((), ())), preferred_element_type=F32)


def _dot_tn(a, b):
    return lax.dot_general(a, b, (((0,), (0,)), ((), ())), preferred_element_type=F32)


def _split2(x):
    h1 = x.astype(BF16)
    return h1, (x - h1.astype(F32)).astype(BF16)


def _dot_exact_rhs(x, m):
    h1, h2 = _split2(x)
    return _dot(h1, m) + _dot(h2, m)


def _dot_exact_rhs_stacked(x, m2):
    return _dot(jnp.concatenate(_split2(x), axis=1), m2)


def _dot_exact_lhs(m, x):
    h1, h2 = _split2(x)
    return _dot(m, h1) + _dot(m, h2)


def _sigmoid(x):
    return 1.0 / (1.0 + jnp.exp(-x))


def _norm_mod(x, g, scale, shift):
    ms = jnp.mean(x * x, axis=-1, keepdims=True)
    return (x * lax.rsqrt(ms + NORM_EPS) * g) * (1.0 + scale) + shift


def _stack_heads(x, lo):
    return jnp.concatenate([jnp.where(lo, x, 0.0), jnp.where(lo, 0.0, x)], axis=0)


def _cond_row(i, tm):
    n_ctx_blocks = N_CTX // tm
    return jnp.where(i < n_ctx_blocks, 0, 1 + (i - n_ctx_blocks) // (DEC_SEQ // tm))


def _const_spec(shape):
    nd = len(shape)
    return pl.BlockSpec(shape, lambda *_: (0,) * nd, pipeline_mode=pl.Buffered(1))


def _layer_spec(shape, l):
    nd = len(shape)
    return pl.BlockSpec((None,) + tuple(shape), lambda *_: (l,) + (0,) * nd, pipeline_mode=pl.Buffered(1))


def _mod_spec(tm, l):
    return pl.BlockSpec((None, None, 1, 6 * D_MODEL), lambda i: (l, _cond_row(i, tm), 0, 0))


def _grouped_index(i):
    n_ctx_blocks = N_CTX // TM_RWKV
    per_seq = DEC_SEQ // TM_RWKV
    in_ctx = i < n_ctx_blocks
    il = jnp.maximum(i - n_ctx_blocks, 0)
    seq_l = il // per_seq
    blk = jnp.where(in_ctx, i // WKV_NS, n_ctx_blocks // WKV_NS + (seq_l // WKV_NS) * per_seq + il % per_seq)
    slot = jnp.where(in_ctx, i % WKV_NS, seq_l % WKV_NS)
    return blk, slot


def _grouped_spec():
    return pl.BlockSpec((CHUNKS_PER_BLOCK, None, WKV_T, D_MODEL), lambda i: (*_grouped_index(i), 0, 0))


def _grouped_spec2():
    return pl.BlockSpec((2, CHUNKS_PER_BLOCK, None, WKV_T, D_MODEL), lambda i: (0, *_grouped_index(i), 0, 0))


def _params(n_axes=1):
    return pltpu.CompilerParams(
        dimension_semantics=("arbitrary",) * n_axes, vmem_limit_bytes=VMEM_LIMIT)


def _ada_kernel(cond_ref, w_ref, b_ref, o_ref):
    c = cond_ref[...]
    s = c * _sigmoid(c)
    o_ref[...] = _dot(s.astype(BF16), w_ref[...].astype(BF16)) + b_ref[...]


def _ada_call(cond, ada_w, ada_b):
    return pl.pallas_call(
        _ada_kernel,
        grid=(DEPTH, 6 * D_MODEL // ADA_TN),
        in_specs=[
            pl.BlockSpec((COND_ROWS, D_MODEL), lambda l, n: (0, 0)),
            pl.BlockSpec((None, D_MODEL, ADA_TN), lambda l, n: (l, 0, n)),
            pl.BlockSpec((None, 1, ADA_TN), lambda l, n: (l, 0, n)),
        ],
        out_specs=pl.BlockSpec((None, COND_ROWS, ADA_TN), lambda l, n: (l, 0, n)),
        out_shape=jax.ShapeDtypeStruct((DEPTH, COND_ROWS, 6 * D_MODEL), F32),
        compiler_params=_params(2),
        name="ada_mod",
    )(cond, ada_w, ada_b.reshape(DEPTH, 1, 6 * D_MODEL))


def _sgu_kernel(split_input, *refs):
    if split_input:
        xa_ref, xb_ref, mod_ref, ng_ref, win_ref, lng_ref, lnb_ref, wp_ref, bmap_ref, wout_ref, o_ref = refs
        x = jnp.where(pl.program_id(0) < N_CTX // TM_SGU, xa_ref[...], xb_ref[...])
    else:
        x_ref, mod_ref, ng_ref, win_ref, lng_ref, lnb_ref, wp_ref, bmap_ref, wout_ref, o_ref = refs
        x = x_ref[...]
    mod = mod_ref[...]
    shift, scale, gate = (mod[:, j * D_MODEL:(j + 1) * D_MODEL] for j in range(3))
    h = _norm_mod(x, ng_ref[...], scale, shift)
    z = _dot(h.astype(BF16), win_ref[...])
    z = z * (0.5 * (1.0 + jnp.tanh(0.7978845608028654 * (z + 0.044715 * (z * z * z)))))
    u = z[:, :D_MODEL]
    v = z[:, D_MODEL:]
    mu = jnp.mean(v, axis=-1, keepdims=True)
    dv = v - mu
    var = jnp.mean(dv * dv, axis=-1, keepdims=True)
    v = dv * lax.rsqrt(var + NORM_EPS) * lng_ref[...] + lnb_ref[...]
    lo = lax.broadcasted_iota(jnp.int32, (CHUNK, LANES), 1) < HEAD_DIM
    bmap = bmap_ref[...]
    n_chunks = TM_SGU // CHUNK
    cols = []
    for p in range(N_PAIRS):
        rhs = jnp.concatenate(
            [_stack_heads(v[c * CHUNK:(c + 1) * CHUNK, p * LANES:(p + 1) * LANES], lo) for c in range(n_chunks)],
            axis=1).astype(BF16)
        mixed = _dot(wp_ref[p], rhs)
        cols.append(jnp.concatenate([mixed[:, c * LANES:(c + 1) * LANES] for c in range(n_chunks)], axis=0))
    vs = jnp.concatenate(cols, axis=1) + jnp.concatenate([bmap] * n_chunks, axis=0)
    m = _dot((u * vs).astype(BF16), wout_ref[...])
    o_ref[...] = x + gate * m


def _sgu_call(xs, mod, l, ng, w_in, ln_g, ln_b, w_pair, bmap, w_out):
    i = l // 2
    tm = TM_SGU
    row = pl.BlockSpec((tm, D_MODEL), lambda b: (b, 0))
    if len(xs) == 2:
        n_ctx_blocks = N_CTX // tm
        x_specs = [pl.BlockSpec((tm, D_MODEL), lambda b: (jnp.minimum(b, n_ctx_blocks - 1), 0)),
                   pl.BlockSpec((tm, D_MODEL), lambda b: (jnp.maximum(b - n_ctx_blocks, 0), 0))]
    else:
        x_specs = [row]
    return pl.pallas_call(
        functools.partial(_sgu_kernel, len(xs) == 2),
        grid=(N_TOK // tm,),
        in_specs=x_specs + [
            _mod_spec(tm, l),
            _layer_spec((1, D_MODEL), l),
            _layer_spec((D_MODEL, 2 * D_MODEL), i),
            _layer_spec((1, D_MODEL), i), _layer_spec((1, D_MODEL), i),
            _layer_spec((N_PAIRS, CHUNK, 2 * CHUNK), i),
            _layer_spec((CHUNK, D_MODEL), i),
            _layer_spec((D_MODEL, D_MODEL), i),
        ],
        out_specs=row,
        out_shape=jax.ShapeDtypeStruct((N_TOK, D_MODEL), F32),
        compiler_params=_params(),
        name="sgu_layer",
    )(*xs, mod, ng, w_in, ln_g, ln_b, w_pair, bmap, w_out)


def _mlp_kernel(x_ref, mod_ref, ng_ref, w1_ref, w2_ref, o_ref):
    x = x_ref[...]
    mod = mod_ref[...]
    shift, scale, gate = (mod[:, j * D_MODEL:(j + 1) * D_MODEL] for j in range(3, 6))
    h = _norm_mod(x, ng_ref[...], scale, shift).astype(BF16)
    acc = jnp.zeros((TM_MLP, D_MODEL), F32)
    for c in range(D_FF // FF_CHUNK):
        a = _dot(h, w1_ref[:, c * FF_CHUNK:(c + 1) * FF_CHUNK])
        a = jnp.square(jnp.maximum(a, 0.0))
        acc = acc + _dot(a.astype(BF16), w2_ref[c * FF_CHUNK:(c + 1) * FF_CHUNK, :])
    o_ref[...] = x + gate * acc


def _mlp_call(x, mod, l, ng, w1, w2):
    row = pl.BlockSpec((TM_MLP, D_MODEL), lambda i: (i, 0))
    return pl.pallas_call(
        _mlp_kernel,
        grid=(N_TOK // TM_MLP,),
        in_specs=[row, _mod_spec(TM_MLP, l), _layer_spec((1, D_MODEL), l),
                  _layer_spec((D_MODEL, D_FF), l), _layer_spec((D_FF, D_MODEL), l)],
        out_specs=row,
        out_shape=jax.ShapeDtypeStruct((N_TOK, D_MODEL), F32),
        compiler_params=_params(),
        name="mlp_layer",
    )(x, mod, ng, w1, w2)


def _proj_kernel(has_vmix, *refs):
    if has_vmix:
        (x_ref, xp_ref, xn_ref, mod_ref, ng_ref, mu_ref, wr_ref, wk_ref, wv_ref, g1_ref, g2_ref,
         w0_ref, w1_ref, w2_ref, a0_ref, a1_ref, a2_ref, ka_ref, rk_ref, e_ref, et_ref,
         vf_ref, v0_ref, v1_ref, v2_ref,
         r_out, k_out, v_out, g_out, bv_out, lw_out, a_out) = refs
    else:
        (x_ref, xp_ref, xn_ref, mod_ref, ng_ref, mu_ref, wr_ref, wk_ref, wv_ref, g1_ref, g2_ref,
         w0_ref, w1_ref, w2_ref, a0_ref, a1_ref, a2_ref, ka_ref, rk_ref, e_ref, et_ref,
         r_out, k_out, v_out, g_out, bv_out, lw_out, a_out) = refs
    tm = TM_RWKV
    grouped = (CHUNKS_PER_BLOCK, WKV_T, D_MODEL)
    i = pl.program_id(0)
    t0 = i * tm
    in_ctx = t0 < N_CTX
    pos = jnp.where(in_ctx, t0 & (SEQ - 1), (t0 - N_CTX) & (DEC_SEQ - 1))
    seq_len = jnp.where(in_ctx, SEQ, DEC_SEQ)
    has_prev = jnp.where(pos != 0, 1.0, 0.0)
    has_next = jnp.where(pos + tm != seq_len, 1.0, 0.0)

    mod = mod_ref[...]
    shift, scale = mod[:, :D_MODEL], mod[:, D_MODEL:2 * D_MODEL]
    ng = ng_ref[...]
    h = _norm_mod(x_ref[...], ng, scale, shift)
    h_prev = _norm_mod(xp_ref[7:8, :], ng, scale, shift) * has_prev
    h_next = _norm_mod(xn_ref[0:1, :], ng, scale, shift) * has_next
    rows = lax.broadcasted_iota(jnp.int32, (tm, D_MODEL), 0)
    h_dn = jnp.where(rows == 0, h_prev, pltpu.roll(h, 1, 0))
    h_up = jnp.where(rows == tm - 1, h_next, pltpu.roll(h, tm - 1, 0))
    xx = 0.5 * (h_dn + h_up) - h
    mu = mu_ref[...]
    xr, xw, xk, xv, xa, xg = ((h + xx * mu[j:j + 1, :]).astype(BF16) for j in range(6))

    r = _dot(xr, wr_ref[...])
    k = _dot(xk, wk_ref[...])
    v = _dot(xv, wv_ref[...])
    if has_vmix:
        mix = _sigmoid(v0_ref[...] + _dot(_dot(xv, v1_ref[...]).astype(BF16), v2_ref[...]))
        v = v + (vf_ref[...].reshape(tm, D_MODEL) - v) * mix
    g = _dot(_sigmoid(_dot(xg, g1_ref[...])).astype(BF16), g2_ref[...])

    lo = lax.broadcasted_iota(jnp.int32, (tm, LANES), 1) < HEAD_DIM
    th = jnp.tanh(_dot(xw, w1_ref[...]))
    al = _dot(xa, a1_ref[...])
    a_sum = None
    for d in range(2):
        keep = lo if d == 0 else jnp.logical_not(lo)
        z = w0_ref[d:d + 1, :] + _dot(jnp.where(keep, th, 0.0).astype(BF16), w2_ref[...])
        lw_out[d] = (-0.6065306597126334 * _sigmoid(z)).reshape(grouped)
        a_d = _sigmoid(a0_ref[d:d + 1, :] + _dot(jnp.where(keep, al, 0.0).astype(BF16), a2_ref[...]))
        a_out[d] = a_d.reshape(grouped)
        a_sum = a_d if a_sum is None else a_sum + a_d
    ksum = k * (2.0 + (a_sum - 2.0) * ka_ref[...])
    head_sum = _dot_exact_rhs(r * ksum * rk_ref[...], e_ref[...])
    bonus = _dot_exact_rhs_stacked(head_sum, et_ref[...])
    r_out[...] = r.reshape(grouped)
    k_out[...] = k.reshape(grouped)
    v_out[...] = v.reshape(grouped)
    g_out[...] = g.reshape(grouped)
    bv_out[...] = (bonus * v).reshape(grouped)


def _proj_call(x, mod, l, ng, W, v_first):
    i = l // 2
    tm = TM_RWKV
    nb8 = tm // 8
    has_vmix = v_first is not None
    row = pl.BlockSpec((tm, D_MODEL), lambda b: (b, 0))
    prev = pl.BlockSpec((8, D_MODEL), lambda b: (jnp.maximum(b * nb8 - 1, 0), 0))
    nxt = pl.BlockSpec((8, D_MODEL), lambda b: (jnp.minimum((b + 1) * nb8, N_TOK // 8 - 1), 0))
    mat = _layer_spec((D_MODEL, D_MODEL), i)
    vec = _layer_spec((1, D_MODEL), i)
    vec2 = _layer_spec((2, D_MODEL), i)
    down = _layer_spec((D_MODEL, LANES), i)
    up = _layer_spec((LANES, D_MODEL), i)
    in_specs = [row, prev, nxt, _mod_spec(tm, l), _layer_spec((1, D_MODEL), l), _layer_spec((6, D_MODEL), i),
                mat, mat, mat, down, up,
                vec2, down, up, vec2, down, up, vec, vec,
                _const_spec((D_MODEL, LANES)), _const_spec((2 * LANES, D_MODEL))]
    args = [x, x, x, mod, ng, W['mu'], W['w_r'], W['w_k'], W['w_v'], W['g1'], W['g2'],
            W['w0'], W['w1'], W['w2'], W['a0'], W['a1'], W['a2'], W['k_a'], W['r_k'], W['e'], W['et']]
    if has_vmix:
        in_specs += [_grouped_spec(), _layer_spec((1, D_MODEL), i - 1), _layer_spec((D_MODEL, LANES), i - 1),
                     _layer_spec((LANES, D_MODEL), i - 1)]
        args += [v_first, W['v0'], W['v1'], W['v2']]
    tok = jax.ShapeDtypeStruct((N_GROUPED, WKV_NS, WKV_T, D_MODEL), F32)
    tok2 = jax.ShapeDtypeStruct((2, N_GROUPED, WKV_NS, WKV_T, D_MODEL), F32)
    return pl.pallas_call(
        functools.partial(_proj_kernel, has_vmix),
        grid=(N_TOK // tm,),
        in_specs=in_specs,
        out_specs=[_grouped_spec()] * 5 + [_grouped_spec2()] * 2,
        out_shape=[tok] * 5 + [tok2] * 2,
        compiler_params=_params(),
        name="rwkv_proj",
    )(*args)


def _wkv_decode(i):
    in_ctx = i < WKV_STEPS_CTX
    il = jnp.maximum(i - WKV_STEPS_CTX, 0)
    grp = jnp.where(in_ctx, i // NC_CTX, BATCH // WKV_NS + il // NC_LAT)
    j = jnp.where(in_ctx, i % NC_CTX, il % NC_LAT)
    nc = jnp.where(in_ctx, NC_CTX, NC_LAT)
    base = jnp.where(in_ctx, (i // NC_CTX) * NC_CTX, WKV_STEPS_CTX + (il // NC_LAT) * NC_LAT)
    return grp, j, nc, base


def _wkv_entry(i, d):
    _, j, nc, base = _wkv_decode(i)
    return base + (j if d == 0 else nc - 1 - j)


def _wkv_masks():
    T = WKV_T
    t = np.arange(T)[:, None]
    s1 = np.arange(T)[None, :]
    s2 = np.arange(2 * T)[None, :] % T
    s4 = np.arange(4 * T)[None, :] % T
    cum, strict, incl, lvl = [], [], [], []
    for d in range(2):
        early = (lambda s: s < t) if d == 0 else (lambda s: s > t)
        cum.append(early(s1) | (s1 == t))
        strict.append(early(s2))
        incl.append(early(s4) | (s4 == t))
        lvl.append([early(s2) & (((t ^ s2) >> lb) == 1) for lb in range(6)])
    hd = np.arange(2 * T) // HEAD_DIM
    return (jnp.asarray(np.array(cum), BF16), jnp.asarray(np.array(strict), F32),
            jnp.asarray(np.array(incl), F32), jnp.asarray(np.array(lvl), F32),
            jnp.asarray(hd[:, None] == hd[None, :], F32))


def _wkv_kernel(rf_ref, kf_ref, vf_ref, lwf_ref, af_ref, rb_ref, kb_ref, vb_ref, lwb_ref, ab_ref,
                kkw_ref, ka_ref, cum_ref, strict_ref, incl_ref, lvl_ref, same_ref, s0_ref,
                yf_ref, yb_ref, sf_ref, s_scr):
    T = WKV_T
    _, j, nc, _ = _wkv_decode(pl.program_id(0))
    lo = lax.broadcasted_iota(jnp.int32, (T, LANES), 1) < HEAD_DIM
    pairs = range(N_PAIRS)
    sls = [slice(p * LANES, (p + 1) * LANES) for p in pairs]
    chains = [(q, d) for q in range(WKV_NS) for d in range(2)]
    units = [(q, d, p) for q, d in chains for p in pairs]

    @pl.when(j == 0)
    def _():
        for q, d, p in units:
            s_scr[q, d, p] = _stack_heads(s0_ref[q, d, :, sls[p]], lo)

    def stk(x):
        return _stack_heads(x.astype(BF16), lo)

    kkw = kkw_ref[...]
    ka = ka_ref[...]
    dir_refs = ((rf_ref, kf_ref, vf_ref, lwf_ref, af_ref), (rb_ref, kb_ref, vb_ref, lwb_ref, ab_ref))
    prep_pieces = 2 * (1 + N_PAIRS // 2)

    def scan(q):
        units = [(d, p) for d in range(2) for p in pairs]
        ar, bk, bkc, v2, vv, c_tot = {}, {}, {}, {}, {}, {}
        for d in range(2):
            r_ref, k_ref, v_ref, lw_ref, a_ref = dir_refs[d]
            lw = lw_ref[q]
            logc = _dot_exact_lhs(cum_ref[d], lw)
            tot = jnp.sum(lw, axis=0, keepdims=True)
            c = jnp.exp(logc)
            c_inv = jnp.exp(-logc)
            c_prev = jnp.exp(logc - lw)
            c_tot[d] = jnp.exp(tot)
            c_end = c_tot[d] * c_inv
            r = r_ref[q]
            k = k_ref[q]
            a = a_ref[q]
            vv[d] = v_ref[q]
            kk_raw = k * kkw
            kd = k * (1.0 + (a - 1.0) * ka)
            yield
            for p in pairs:
                sl = sls[p]
                kq = kk_raw[:, sl]
                sq = kq * kq
                n_lo = jnp.sum(jnp.where(lo, sq, 0.0), axis=-1, keepdims=True)
                n_hi = jnp.sum(jnp.where(lo, 0.0, sq), axis=-1, keepdims=True)
                kk = kq * lax.rsqrt(jnp.maximum(jnp.where(lo, n_lo, n_hi), 1e-24))
                beta = kk * a[:, sl]
                u = (d, p)
                ar[u] = jnp.concatenate([-kk * c_prev[:, sl], r[:, sl] * c[:, sl]], axis=0).astype(BF16)
                bk[u] = jnp.concatenate([stk(beta * c_inv[:, sl]), stk(kd[:, sl] * c_inv[:, sl])], axis=0)
                bkc[u] = jnp.concatenate([beta * c_end[:, sl], kd[:, sl] * c_end[:, sl]], axis=0).astype(BF16)
                v2[u] = stk(vv[d][:, sl])
                if p % 2 == 1:
                    yield
        gm = {u: _dot_nt(ar[u], bk[u]) for u in units}
        yield
        uy0 = {(d, p): _dot_nt(ar[d, p], s_scr[q, d, p].astype(BF16)) for d, p in units}
        yield
        kv_mask = {d: jnp.concatenate([strict_ref[d], incl_ref[d][:, 2 * T:]], axis=0) for d in range(2)}
        xv = {u: _dot((gm[u][:, 2 * T:] * kv_mask[u[0]]).astype(BF16), v2[u]) for u in units}
        x = {u: uy0[u][:T] + xv[u][:T] for u in units}
        yield
        w = {u: gm[u][:T, :2 * T] * lvl_ref[u[0], 0] for u in units}
        for lb in range(1, 6):
            p_off = {u: gm[u][:T, :2 * T] * lvl_ref[u[0], lb] for u in units}
            t1 = {u: p_off[u] + _dot(p_off[u].astype(BF16), stk(w[u])) for u in units}
            yield
            w = {u: w[u] + t1[u] + _dot(w[u].astype(BF16), stk(t1[u])) for u in units}
            yield
        uu = {u: x[u] + _dot(w[u].astype(BF16), stk(x[u])) for u in units}
        yield
        for d, p in units:
            y_ref = yf_ref if d == 0 else yb_ref
            rb = (gm[d, p][T:, :2 * T] * incl_ref[d][:, :2 * T]).astype(BF16)
            y_ref[q, :, sls[p]] = uy0[d, p][T:] + xv[d, p][T:] + _dot(rb, stk(uu[d, p]))
        yield
        for d, p in units:
            uv = jnp.concatenate([uu[d, p], vv[d][:, sls[p]]], axis=0).astype(BF16)
            s_scr[q, d, p] = (s_scr[q, d, p] * c_tot[d][:, sls[p]]
                              + same_ref[...] * _dot_tn(uv, bkc[d, p]))

    scans = [scan(q) for q in range(WKV_NS)]
    live = [True] * WKV_NS
    tick = 0
    while any(live):
        for q in range(WKV_NS):
            if live[q] and tick >= q * prep_pieces:
                live[q] = next(scans[q], "done") != "done"
        tick += 1

    @pl.when(j == nc - 1)
    def _():
        for q, d, p in units:
            sf_ref[q, d, :, sls[p]] = s_scr[q, d, p, :HEAD_DIM, :] + s_scr[q, d, p, HEAD_DIM:, :]


def _wkv_call(r, k, v, lw, a, kkw, ka, s0, i):
    blk = (WKV_NS, WKV_T, D_MODEL)
    tok = [pl.BlockSpec((None,) + blk, functools.partial(lambda d, b: (_wkv_entry(b, d), 0, 0, 0), d))
           for d in range(2)]
    tok_d = [pl.BlockSpec((None, None) + blk, functools.partial(lambda d, b: (d, _wkv_entry(b, d), 0, 0, 0), d))
             for d in range(2)]
    state = pl.BlockSpec((WKV_NS, 2, HEAD_DIM, D_MODEL), lambda b: (_wkv_decode(b)[0], 0, 0, 0))
    vec = _layer_spec((1, D_MODEL), i)
    masks = _wkv_masks()
    y_shape = jax.ShapeDtypeStruct((N_GROUPED,) + blk, F32)
    return pl.pallas_call(
        _wkv_kernel,
        grid=(WKV_STEPS,),
        in_specs=[tok[0], tok[0], tok[0], tok_d[0], tok_d[0], tok[1], tok[1], tok[1], tok_d[1], tok_d[1],
                  vec, vec] + [_const_spec(m.shape) for m in masks] + [state],
        out_specs=[tok[0], tok[1], state],
        out_shape=[y_shape, y_shape, jax.ShapeDtypeStruct((N_SEQ, 2, HEAD_DIM, D_MODEL), F32)],
        scratch_shapes=[pltpu.VMEM((WKV_NS, 2, N_PAIRS, 2 * HEAD_DIM, LANES), F32)],
        compiler_params=_params(),
        name="wkv_scan",
    )(r, k, v, lw, a, r, k, v, lw, a, kkw, ka, *masks, s0)


def _post_kernel(x_ref, mod_ref, yf_ref, yb_ref, bv_ref, g_ref, lng_ref, lnb_ref, e_ref, et_ref, wo_ref,
                 o_ref):
    rows = (TM_RWKV, D_MODEL)
    gate = mod_ref[...][:, 2 * D_MODEL:3 * D_MODEL]
    y = (yf_ref[...] + yb_ref[...]).reshape(rows)
    e = e_ref[...]
    et = et_ref[...]
    inv_n = 1.0 / HEAD_DIM
    mu = _dot_exact_rhs_stacked(_dot_exact_rhs(y, e), et) * inv_n
    dv = y - mu
    var = _dot_exact_rhs_stacked(_dot_exact_rhs(dv * dv, e), et) * inv_n
    yn = dv * lax.rsqrt(var + GN_EPS) * lng_ref[...] + lnb_ref[...]
    yy = (yn + bv_ref[...].reshape(rows)) * g_ref[...].reshape(rows)
    o_ref[...] = x_ref[...] + gate * _dot(yy.astype(BF16), wo_ref[...])


def _post_call(x, mod, l, yf, yb, bv, g, ln_g, ln_b, e, et, w_o):
    i = l // 2
    tm = TM_RWKV
    row = pl.BlockSpec((tm, D_MODEL), lambda b: (b, 0))
    grouped = _grouped_spec()
    vec = _layer_spec((1, D_MODEL), i)
    return pl.pallas_call(
        _post_kernel,
        grid=(N_TOK // tm,),
        in_specs=[row, _mod_spec(tm, l), grouped, grouped, grouped, grouped, vec, vec,
                  _const_spec((D_MODEL, LANES)), _const_spec((2 * LANES, D_MODEL)),
                  _layer_spec((D_MODEL, D_MODEL), i)],
        out_specs=row,
        out_shape=jax.ShapeDtypeStruct((N_TOK, D_MODEL), F32),
        compiler_params=_params(),
        name="rwkv_out",
    )(x, mod, yf, yb, bv, g, ln_g, ln_b, e, et, w_o)


def _final_kernel(x_ref, g_ref, o_ref):
    x = x_ref[...]
    ms = jnp.mean(x * x, axis=-1, keepdims=True)
    o_ref[...] = x * lax.rsqrt(ms + NORM_EPS) * g_ref[...]


def _final_call(x, g, row0, n_rows):
    tm = TM_NORM
    b0 = row0 // tm
    return pl.pallas_call(
        _final_kernel,
        grid=(n_rows // tm,),
        in_specs=[pl.BlockSpec((tm, D_MODEL), lambda i: (i + b0, 0)), _const_spec((1, D_MODEL))],
        out_specs=pl.BlockSpec((tm, D_MODEL), lambda i: (i, 0)),
        out_shape=jax.ShapeDtypeStruct((n_rows, D_MODEL), F32),
        compiler_params=_params(),
        name="final_norm",
    )(x, g)


def _pad_cols(w, n):
    return jnp.pad(w, [(0, 0)] * (w.ndim - 1) + [(0, n - w.shape[-1])])


def _pad_rows(w, n):
    return jnp.pad(w, [(0, 0)] * (w.ndim - 2) + [(0, n - w.shape[-2]), (0, 0)])


def kernel(x_prompt, x_sample, state_wkv, c, c_ctx, norm1_g, norm2_g, ada_w, ada_b, sgu_w_in, sgu_ln_g,
           sgu_ln_b, sgu_w_s, sgu_b_s, sgu_w_out, rwkv_mu, rwkv_w_r, rwkv_w_k, rwkv_w_v, rwkv_w_o,
           rwkv_w0, rwkv_w1, rwkv_w2, rwkv_a0, rwkv_a1, rwkv_a2, rwkv_v0, rwkv_v1, rwkv_v2, rwkv_g1,
           rwkv_g2, rwkv_k_k, rwkv_k_a, rwkv_r_k, rwkv_ln_g, rwkv_ln_b, mlp_w1, mlp_w2, final_g):
    D = D_MODEL
    bf = lambda w: w.astype(BF16)
    vec = lambda w: w.reshape(w.shape[0], 1, D)

    cond = jnp.concatenate([c_ctx[None, :], c, jnp.zeros((COND_ROWS - 1 - DEC_BATCH, D), F32)], axis=0)
    mod = _ada_call(cond, ada_w, ada_b).reshape(DEPTH, COND_ROWS, 1, 6 * D)

    head_of = jnp.arange(D) // HEAD_DIM
    e = (head_of[:, None] == jnp.arange(LANES)[None, :]).astype(BF16)
    W = dict(
        mu=rwkv_mu, w_r=bf(rwkv_w_r), w_k=bf(rwkv_w_k), w_v=bf(rwkv_w_v),
        g1=bf(rwkv_g1), g2=bf(rwkv_g2),
        w0=rwkv_w0,
        w1=bf(jnp.concatenate([rwkv_w1[:, 0], rwkv_w1[:, 1]], axis=-1)),
        w2=bf(jnp.concatenate([rwkv_w2[:, 0], rwkv_w2[:, 1]], axis=-2)),
        a0=rwkv_a0,
        a1=bf(jnp.concatenate([rwkv_a1[:, 0], rwkv_a1[:, 1]], axis=-1)),
        a2=bf(jnp.concatenate([rwkv_a2[:, 0], rwkv_a2[:, 1]], axis=-2)),
        v0=vec(rwkv_v0), v1=bf(_pad_cols(rwkv_v1, LANES)), v2=bf(_pad_rows(rwkv_v2, LANES)),
        k_a=vec(rwkv_k_a), r_k=rwkv_r_k.reshape(N_RWKV, 1, D), e=e, et=jnp.concatenate([e.T, e.T], axis=0),
    )
    n1 = vec(norm1_g)
    n2 = vec(norm2_g)
    w_pair = bf(sgu_w_s.reshape(-1, N_PAIRS, 2, CHUNK, CHUNK).transpose(0, 1, 3, 2, 4)
                .reshape(-1, N_PAIRS, CHUNK, 2 * CHUNK))
    bmap = jnp.repeat(jnp.swapaxes(sgu_b_s, 1, 2), D // SGU_GROUPS, axis=2)
    sgu_w_in_b, sgu_w_out_b = bf(sgu_w_in), bf(sgu_w_out)
    mlp_w1_b, mlp_w2_b = bf(mlp_w1), bf(mlp_w2)
    w_o_b = bf(rwkv_w_o)
    k_k, ln_g, ln_b = vec(rwkv_k_k), vec(rwkv_ln_g), vec(rwkv_ln_b)
    sgu_g, sgu_b = vec(sgu_ln_g), vec(sgu_ln_b)

    s_lat = state_wkv.transpose(1, 0, 2, 4, 3, 5).reshape(N_RWKV, DEC_BATCH, 2, HEAD_DIM, D)
    s_init = jnp.concatenate([jnp.zeros((N_RWKV, BATCH, 2, HEAD_DIM, D), F32), s_lat], axis=1)

    xs = (x_prompt.reshape(N_CTX, D), x_sample.reshape(N_LAT, D))
    v_first = None
    finals = []
    for l in range(DEPTH):
        i = l // 2
        if l % 2 == 0:
            x = _sgu_call(xs, mod, l, n1, sgu_w_in_b, sgu_g, sgu_b, w_pair, bmap, sgu_w_out_b)
        else:
            r, k, v, g, bv, lw, a = _proj_call(x, mod, l, n1, W, v_first)
            if v_first is None:
                v_first = v
            yf, yb, s_fin = _wkv_call(r, k, v, lw, a, k_k, W['k_a'], s_init[i], i)
            finals.append(s_fin[:BATCH])
            x = _post_call(x, mod, l, yf, yb, bv, g, ln_g, ln_b, W['e'], W['et'], w_o_b)
        x = _mlp_call(x, mod, l, n2, mlp_w1_b, mlp_w2_b)
        xs = (x,)

    y_prompt = _final_call(x, final_g.reshape(1, D), 0, N_CTX).reshape(BATCH, SEQ, D)
    y_sample = _final_call(x, final_g.reshape(1, D), N_CTX, N_LAT).reshape(DEC_BATCH, DEC_SEQ, D)
    new_state = jnp.stack(finals, axis=1).reshape(BATCH, N_RWKV, 2, HEAD_DIM, N_HEADS, HEAD_DIM)
    new_state = new_state.transpose(0, 1, 2, 4, 3, 5)
    return (y_prompt, y_sample, new_state)
```

```python
import functools

import numpy as np
import jax
import jax.numpy as jnp
from jax import lax
from jax.experimental import pallas as pl
from jax.experimental.pallas import tpu as pltpu

D_MODEL = 1024
BATCH = 16
SEQ = 256
DEPTH = 4
DEC_BATCH = 8
DEC_SEQ = 2048
N_RWKV = DEPTH // 2
CHUNK = 128
SGU_GROUPS = 16
HEAD_DIM = 64
N_HEADS = D_MODEL // HEAD_DIM
LORA_V = 32
D_FF = 4 * D_MODEL
NORM_EPS = 1e-6
GN_EPS = 64e-5

LANES = 128
N_PAIRS = D_MODEL // LANES
N_CTX = BATCH * SEQ
N_LAT = DEC_BATCH * DEC_SEQ
N_TOK = N_CTX + N_LAT
N_SEQ = BATCH + DEC_BATCH
COND_ROWS = 16

WKV_T = 64
WKV_NS = 4
NC_CTX = SEQ // WKV_T
NC_LAT = DEC_SEQ // WKV_T
WKV_STEPS_CTX = (BATCH // WKV_NS) * NC_CTX
WKV_STEPS = WKV_STEPS_CTX + (DEC_BATCH // WKV_NS) * NC_LAT

N_GROUPED = N_TOK // (WKV_NS * WKV_T)
TM_RWKV = 256
CHUNKS_PER_BLOCK = TM_RWKV // WKV_T

TM_SGU = 512
TM_MLP = 512
TM_NORM = 512
FF_CHUNK = 1024
ADA_TN = 1536
VMEM_LIMIT = 56 * 1024 * 1024

F32 = jnp.float32
BF16 = jnp.bfloat16


def _dot(a, b):
    return jnp.dot(a, b, preferred_element_type=F32)


def _dot_nt(a, b):
    return lax.dot_general(a, b, (((1,), (1,)), ((), ())), preferred_element_type=F32)


def _dot_tn(a, b):
    return lax.dot_general(a, b, (((0,), (0,)), ((), ())), preferred_element_type=F32)


def _split2(x):
    h1 = x.astype(BF16)
    return h1, (x - h1.astype(F32)).astype(BF16)


def _dot_exact_rhs(x, m):
    h1, h2 = _split2(x)
    return _dot(h1, m) + _dot(h2, m)


def _dot_exact_rhs_stacked(x, m2):
    return _dot(jnp.concatenate(_split2(x), axis=1), m2)


def _dot_exact_lhs(m, x):
    h1, h2 = _split2(x)
    return _dot(m, h1) + _dot(m, h2)


def _sigmoid(x):
    return 1.0 / (1.0 + jnp.exp(-x))


def _norm_mod(x, g, scale, shift):
    ms = jnp.mean(x * x, axis=-1, keepdims=True)
    return (x * lax.rsqrt(ms + NORM_EPS) * g) * (1.0 + scale) + shift


def _stack_heads(x, lo):
    return jnp.concatenate([jnp.where(lo, x, 0.0), jnp.where(lo, 0.0, x)], axis=0)


def _cond_row(i, tm):
    n_ctx_blocks = N_CTX // tm
    return jnp.where(i < n_ctx_blocks, 0, 1 + (i - n_ctx_blocks) // (DEC_SEQ // tm))


def _const_spec(shape):
    nd = len(shape)
    return pl.BlockSpec(shape, lambda *_: (0,) * nd, pipeline_mode=pl.Buffered(1))


def _layer_spec(shape, l):
    nd = len(shape)
    return pl.BlockSpec((None,) + tuple(shape), lambda *_: (l,) + (0,) * nd, pipeline_mode=pl.Buffered(1))


def _mod_spec(tm, l):
    return pl.BlockSpec((None, None, 1, 6 * D_MODEL), lambda i: (l, _cond_row(i, tm), 0, 0))


def _grouped_index(i):
    n_ctx_blocks = N_CTX // TM_RWKV
    per_seq = DEC_SEQ // TM_RWKV
    in_ctx = i < n_ctx_blocks
    il = jnp.maximum(i - n_ctx_blocks, 0)
    seq_l = il // per_seq
    blk = jnp.where(in_ctx, i // WKV_NS, n_ctx_blocks // WKV_NS + (seq_l // WKV_NS) * per_seq + il % per_seq)
    slot = jnp.where(in_ctx, i % WKV_NS, seq_l % WKV_NS)
    return blk, slot


def _grouped_spec():
    return pl.BlockSpec((CHUNKS_PER_BLOCK, None, WKV_T, D_MODEL), lambda i: (*_grouped_index(i), 0, 0))


def _grouped_spec2():
    return pl.BlockSpec((2, CHUNKS_PER_BLOCK, None, WKV_T, D_MODEL), lambda i: (0, *_grouped_index(i), 0, 0))


def _params(n_axes=1):
    return pltpu.CompilerParams(
        dimension_semantics=("arbitrary",) * n_axes, vmem_limit_bytes=VMEM_LIMIT)


def _ada_kernel(cond_ref, w_ref, b_ref, o_ref):
    c = cond_ref[...]
    s = c * _sigmoid(c)
    o_ref[...] = _dot(s.astype(BF16), w_ref[...].astype(BF16)) + b_ref[...]


def _ada_call(cond, ada_w, ada_b):
    return pl.pallas_call(
        _ada_kernel,
        grid=(DEPTH, 6 * D_MODEL // ADA_TN),
        in_specs=[
            pl.BlockSpec((COND_ROWS, D_MODEL), lambda l, n: (0, 0)),
            pl.BlockSpec((None, D_MODEL, ADA_TN), lambda l, n: (l, 0, n)),
            pl.BlockSpec((None, 1, ADA_TN), lambda l, n: (l, 0, n)),
        ],
        out_specs=pl.BlockSpec((None, COND_ROWS, ADA_TN), lambda l, n: (l, 0, n)),
        out_shape=jax.ShapeDtypeStruct((DEPTH, COND_ROWS, 6 * D_MODEL), F32),
        compiler_params=_params(2),
        name="ada_mod",
    )(cond, ada_w, ada_b.reshape(DEPTH, 1, 6 * D_MODEL))


def _sgu_kernel(split_input, *refs):
    if split_input:
        xa_ref, xb_ref, mod_ref, ng_ref, win_ref, lng_ref, lnb_ref, wp_ref, bmap_ref, wout_ref, o_ref = refs
        x = jnp.where(pl.program_id(0) < N_CTX // TM_SGU, xa_ref[...], xb_ref[...])
    else:
        x_ref, mod_ref, ng_ref, win_ref, lng_ref, lnb_ref, wp_ref, bmap_ref, wout_ref, o_ref = refs
        x = x_ref[...]
    mod = mod_ref[...]
    shift, scale, gate = (mod[:, j * D_MODEL:(j + 1) * D_MODEL] for j in range(3))
    h = _norm_mod(x, ng_ref[...], scale, shift)
    z = _dot(h.astype(BF16), win_ref[...])
    z = z * (0.5 * (1.0 + jnp.tanh(0.7978845608028654 * (z + 0.044715 * (z * z * z)))))
    u = z[:, :D_MODEL]
    v = z[:, D_MODEL:]
    mu = jnp.mean(v, axis=-1, keepdims=True)
    dv = v - mu
    var = jnp.mean(dv * dv, axis=-1, keepdims=True)
    v = dv * lax.rsqrt(var + NORM_EPS) * lng_ref[...] + lnb_ref[...]
    lo = lax.broadcasted_iota(jnp.int32, (CHUNK, LANES), 1) < HEAD_DIM
    bmap = bmap_ref[...]
    n_chunks = TM_SGU // CHUNK
    cols = []
    for p in range(N_PAIRS):
        rhs = jnp.concatenate(
            [_stack_heads(v[c * CHUNK:(c + 1) * CHUNK, p * LANES:(p + 1) * LANES], lo) for c in range(n_chunks)],
            axis=1).astype(BF16)
        mixed = _dot(wp_ref[p], rhs)
        cols.append(jnp.concatenate([mixed[:, c * LANES:(c + 1) * LANES] for c in range(n_chunks)], axis=0))
    vs = jnp.concatenate(cols, axis=1) + jnp.concatenate([bmap] * n_chunks, axis=0)
    m = _dot((u * vs).astype(BF16), wout_ref[...])
    o_ref[...] = x + gate * m


def _sgu_call(xs, mod, l, ng, w_in, ln_g, ln_b, w_pair, bmap, w_out):
    i = l // 2
    tm = TM_SGU
    row = pl.BlockSpec((tm, D_MODEL), lambda b: (b, 0))
    if len(xs) == 2:
        n_ctx_blocks = N_CTX // tm
        x_specs = [pl.BlockSpec((tm, D_MODEL), lambda b: (jnp.minimum(b, n_ctx_blocks - 1), 0)),
                   pl.BlockSpec((tm, D_MODEL), lambda b: (jnp.maximum(b - n_ctx_blocks, 0), 0))]
    else:
        x_specs = [row]
    return pl.pallas_call(
        functools.partial(_sgu_kernel, len(xs) == 2),
        grid=(N_TOK // tm,),
        in_specs=x_specs + [
            _mod_spec(tm, l),
            _layer_spec((1, D_MODEL), l),
            _layer_spec((D_MODEL, 2 * D_MODEL), i),
            _layer_spec((1, D_MODEL), i), _layer_spec((1, D_MODEL), i),
            _layer_spec((N_PAIRS, CHUNK, 2 * CHUNK), i),
            _layer_spec((CHUNK, D_MODEL), i),
            _layer_spec((D_MODEL, D_MODEL), i),
        ],
        out_specs=row,
        out_shape=jax.ShapeDtypeStruct((N_TOK, D_MODEL), F32),
        compiler_params=_params(),
        name="sgu_layer",
    )(*xs, mod, ng, w_in, ln_g, ln_b, w_pair, bmap, w_out)


def _mlp_kernel(x_ref, mod_ref, ng_ref, w1_ref, w2_ref, o_ref):
    x = x_ref[...]
    mod = mod_ref[...]
    shift, scale, gate = (mod[:, j * D_MODEL:(j + 1) * D_MODEL] for j in range(3, 6))
    h = _norm_mod(x, ng_ref[...], scale, shift).astype(BF16)
    acc = jnp.zeros((TM_MLP, D_MODEL), F32)
    for c in range(D_FF // FF_CHUNK):
        a = _dot(h, w1_ref[:, c * FF_CHUNK:(c + 1) * FF_CHUNK])
        a = jnp.square(jnp.maximum(a, 0.0))
        acc = acc + _dot(a.astype(BF16), w2_ref[c * FF_CHUNK:(c + 1) * FF_CHUNK, :])
    o_ref[...] = x + gate * acc


def _mlp_call(x, mod, l, ng, w1, w2):
    row = pl.BlockSpec((TM_MLP, D_MODEL), lambda i: (i, 0))
    return pl.pallas_call(
        _mlp_kernel,
        grid=(N_TOK // TM_MLP,),
        in_specs=[row, _mod_spec(TM_MLP, l), _layer_spec((1, D_MODEL), l),
                  _layer_spec((D_MODEL, D_FF), l), _layer_spec((D_FF, D_MODEL), l)],
        out_specs=row,
        out_shape=jax.ShapeDtypeStruct((N_TOK, D_MODEL), F32),
        compiler_params=_params(),
        name="mlp_layer",
    )(x, mod, ng, w1, w2)


def _proj_kernel(has_vmix, *refs):
    if has_vmix:
        (x_ref, xp_ref, xn_ref, mod_ref, ng_ref, mu_ref, wr_ref, wk_ref, wv_ref, g1_ref, g2_ref,
         w0_ref, w1_ref, w2_ref, a0_ref, a1_ref, a2_ref, ka_ref, rk_ref, e_ref, et_ref,
         vf_ref, v0_ref, v1_ref, v2_ref,
         r_out, k_out, v_out, g_out, bv_out, lw_out, a_out) = refs
    else:
        (x_ref, xp_ref, xn_ref, mod_ref, ng_ref, mu_ref, wr_ref, wk_ref, wv_ref, g1_ref, g2_ref,
         w0_ref, w1_ref, w2_ref, a0_ref, a1_ref, a2_ref, ka_ref, rk_ref, e_ref, et_ref,
         r_out, k_out, v_out, g_out, bv_out, lw_out, a_out) = refs
    tm = TM_RWKV
    grouped = (CHUNKS_PER_BLOCK, WKV_T, D_MODEL)
    i = pl.program_id(0)
    t0 = i * tm
    in_ctx = t0 < N_CTX
    pos = jnp.where(in_ctx, t0 & (SEQ - 1), (t0 - N_CTX) & (DEC_SEQ - 1))
    seq_len = jnp.where(in_ctx, SEQ, DEC_SEQ)
    has_prev = jnp.where(pos != 0, 1.0, 0.0)
    has_next = jnp.where(pos + tm != seq_len, 1.0, 0.0)

    mod = mod_ref[...]
    shift, scale = mod[:, :D_MODEL], mod[:, D_MODEL:2 * D_MODEL]
    ng = ng_ref[...]
    h = _norm_mod(x_ref[...], ng, scale, shift)
    h_prev = _norm_mod(xp_ref[7:8, :], ng, scale, shift) * has_prev
    h_next = _norm_mod(xn_ref[0:1, :], ng, scale, shift) * has_next
    rows = lax.broadcasted_iota(jnp.int32, (tm, D_MODEL), 0)
    h_dn = jnp.where(rows == 0, h_prev, pltpu.roll(h, 1, 0))
    h_up = jnp.where(rows == tm - 1, h_next, pltpu.roll(h, tm - 1, 0))
    xx = 0.5 * (h_dn + h_up) - h
    mu = mu_ref[...]
    xr, xw, xk, xv, xa, xg = ((h + xx * mu[j:j + 1, :]).astype(BF16) for j in range(6))

    r = _dot(xr, wr_ref[...])
    k = _dot(xk, wk_ref[...])
    v = _dot(xv, wv_ref[...])
    if has_vmix:
        mix = _sigmoid(v0_ref[...] + _dot(_dot(xv, v1_ref[...]).astype(BF16), v2_ref[...]))
        v = v + (vf_ref[...].reshape(tm, D_MODEL) - v) * mix
    g = _dot(_sigmoid(_dot(xg, g1_ref[...])).astype(BF16), g2_ref[...])

    lo = lax.broadcasted_iota(jnp.int32, (tm, LANES), 1) < HEAD_DIM
    th = jnp.tanh(_dot(xw, w1_ref[...]))
    al = _dot(xa, a1_ref[...])
    a_sum = None
    for d in range(2):
        keep = lo if d == 0 else jnp.logical_not(lo)
        z = w0_ref[d:d + 1, :] + _dot(jnp.where(keep, th, 0.0).astype(BF16), w2_ref[...])
        lw_out[d] = (-0.6065306597126334 * _sigmoid(z)).reshape(grouped)
        a_d = _sigmoid(a0_ref[d:d + 1, :] + _dot(jnp.where(keep, al, 0.0).astype(BF16), a2_ref[...]))
        a_out[d] = a_d.reshape(grouped)
        a_sum = a_d if a_sum is None else a_sum + a_d
    ksum = k * (2.0 + (a_sum - 2.0) * ka_ref[...])
    head_sum = _dot_exact_rhs(r * ksum * rk_ref[...], e_ref[...])
    bonus = _dot_exact_rhs_stacked(head_sum, et_ref[...])
    r_out[...] = r.reshape(grouped)
    k_out[...] = k.reshape(grouped)
    v_out[...] = v.reshape(grouped)
    g_out[...] = g.reshape(grouped)
    bv_out[...] = (bonus * v).reshape(grouped)


def _proj_call(x, mod, l, ng, W, v_first):
    i = l // 2
    tm = TM_RWKV
    nb8 = tm // 8
    has_vmix = v_first is not None
    row = pl.BlockSpec((tm, D_MODEL), lambda b: (b, 0))
    prev = pl.BlockSpec((8, D_MODEL), lambda b: (jnp.maximum(b * nb8 - 1, 0), 0))
    nxt = pl.BlockSpec((8, D_MODEL), lambda b: (jnp.minimum((b + 1) * nb8, N_TOK // 8 - 1), 0))
    mat = _layer_spec((D_MODEL, D_MODEL), i)
    vec = _layer_spec((1, D_MODEL), i)
    vec2 = _layer_spec((2, D_MODEL), i)
    down = _layer_spec((D_MODEL, LANES), i)
    up = _layer_spec((LANES, D_MODEL), i)
    in_specs = [row, prev, nxt, _mod_spec(tm, l), _layer_spec((1, D_MODEL), l), _layer_spec((6, D_MODEL), i),
                mat, mat, mat, down, up,
                vec2, down, up, vec2, down, up, vec, vec,
                _const_spec((D_MODEL, LANES)), _const_spec((2 * LANES, D_MODEL))]
    args = [x, x, x, mod, ng, W['mu'], W['w_r'], W['w_k'], W['w_v'], W['g1'], W['g2'],
            W['w0'], W['w1'], W['w2'], W['a0'], W['a1'], W['a2'], W['k_a'], W['r_k'], W['e'], W['et']]
    if has_vmix:
        in_specs += [_grouped_spec(), _layer_spec((1, D_MODEL), i - 1), _layer_spec((D_MODEL, LANES), i - 1),
                     _layer_spec((LANES, D_MODEL), i - 1)]
        args += [v_first, W['v0'], W['v1'], W['v2']]
    tok = jax.ShapeDtypeStruct((N_GROUPED, WKV_NS, WKV_T, D_MODEL), F32)
    tok2 = jax.ShapeDtypeStruct((2, N_GROUPED, WKV_NS, WKV_T, D_MODEL), F32)
    return pl.pallas_call(
        functools.partial(_proj_kernel, has_vmix),
        grid=(N_TOK // tm,),
        in_specs=in_specs,
        out_specs=[_grouped_spec()] * 5 + [_grouped_spec2()] * 2,
        out_shape=[tok] * 5 + [tok2] * 2,
        compiler_params=_params(),
        name="rwkv_proj",
    )(*args)


def _wkv_decode(i):
    in_ctx = i < WKV_STEPS_CTX
    il = jnp.maximum(i - WKV_STEPS_CTX, 0)
    grp = jnp.where(in_ctx, i // NC_CTX, BATCH // WKV_NS + il // NC_LAT)
    j = jnp.where(in_ctx, i % NC_CTX, il % NC_LAT)
    nc = jnp.where(in_ctx, NC_CTX, NC_LAT)
    base = jnp.where(in_ctx, (i // NC_CTX) * NC_CTX, WKV_STEPS_CTX + (il // NC_LAT) * NC_LAT)
    return grp, j, nc, base


def _wkv_entry(i, d):
    _, j, nc, base = _wkv_decode(i)
    return base + (j if d == 0 else nc - 1 - j)


def _wkv_masks():
    T = WKV_T
    t = np.arange(T)[:, None]
    s1 = np.arange(T)[None, :]
    s2 = np.arange(2 * T)[None, :] % T
    s4 = np.arange(4 * T)[None, :] % T
    cum, strict, incl, lvl = [], [], [], []
    for d in range(2):
        early = (lambda s: s < t) if d == 0 else (lambda s: s > t)
        cum.append(early(s1) | (s1 == t))
        strict.append(early(s2))
        incl.append(early(s4) | (s4 == t))
        lvl.append([early(s2) & (((t ^ s2) >> lb) == 1) for lb in range(6)])
    hd = np.arange(2 * T) // HEAD_DIM
    return (jnp.asarray(np.array(cum), BF16), jnp.asarray(np.array(strict), F32),
            jnp.asarray(np.array(incl), F32), jnp.asarray(np.array(lvl), F32),
            jnp.asarray(hd[:, None] == hd[None, :], F32))


def _wkv_kernel(rf_ref, kf_ref, vf_ref, lwf_ref, af_ref, rb_ref, kb_ref, vb_ref, lwb_ref, ab_ref,
                kkw_ref, ka_ref, cum_ref, strict_ref, incl_ref, lvl_ref, same_ref, s0_ref,
                yf_ref, yb_ref, sf_ref, s_scr):
    T = WKV_T
    _, j, nc, _ = _wkv_decode(pl.program_id(0))
    lo = lax.broadcasted_iota(jnp.int32, (T, LANES), 1) < HEAD_DIM
    pairs = range(N_PAIRS)
    sls = [slice(p * LANES, (p + 1) * LANES) for p in pairs]
    chains = [(q, d) for q in range(WKV_NS) for d in range(2)]
    units = [(q, d, p) for q, d in chains for p in pairs]

    @pl.when(j == 0)
    def _():
        for q, d, p in units:
            s_scr[q, d, p] = _stack_heads(s0_ref[q, d, :, sls[p]], lo)

    def stk(x):
        return _stack_heads(x.astype(BF16), lo)

    kkw = kkw_ref[...]
    ka = ka_ref[...]
    dir_refs = ((rf_ref, kf_ref, vf_ref, lwf_ref, af_ref), (rb_ref, kb_ref, vb_ref, lwb_ref, ab_ref))
    prep_pieces = 2 * (1 + N_PAIRS // 2)

    def scan(q):
        units = [(d, p) for d in range(2) for p in pairs]
        ar, bk, bkc, v2, vv, c_tot = {}, {}, {}, {}, {}, {}
        for d in range(2):
            r_ref, k_ref, v_ref, lw_ref, a_ref = dir_refs[d]
            lw = lw_ref[q]
            logc = _dot_exact_lhs(cum_ref[d], lw)
            tot = jnp.sum(lw, axis=0, keepdims=True)
            c = jnp.exp(logc)
            c_inv = jnp.exp(-logc)
            c_prev = jnp.exp(logc - lw)
            c_tot[d] = jnp.exp(tot)
            c_end = c_tot[d] * c_inv
            r = r_ref[q]
            k = k_ref[q]
            a = a_ref[q]
            vv[d] = v_ref[q]
            kk_raw = k * kkw
            kd = k * (1.0 + (a - 1.0) * ka)
            yield
            for p in pairs:
                sl = sls[p]
                kq = kk_raw[:, sl]
                sq = kq * kq
                n_lo = jnp.sum(jnp.where(lo, sq, 0.0), axis=-1, keepdims=True)
                n_hi = jnp.sum(jnp.where(lo, 0.0, sq), axis=-1, keepdims=True)
                kk = kq * lax.rsqrt(jnp.maximum(jnp.where(lo, n_lo, n_hi), 1e-24))
                beta = kk * a[:, sl]
                u = (d, p)
                ar[u] = jnp.concatenate([-kk * c_prev[:, sl], r[:, sl] * c[:, sl]], axis=0).astype(BF16)
                bk[u] = jnp.concatenate([stk(beta * c_inv[:, sl]), stk(kd[:, sl] * c_inv[:, sl])], axis=0)
                bkc[u] = jnp.concatenate([beta * c_end[:, sl], kd[:, sl] * c_end[:, sl]], axis=0).astype(BF16)
                v2[u] = stk(vv[d][:, sl])
                if p % 2 == 1:
                    yield
        gm = {u: _dot_nt(ar[u], bk[u]) for u in units}
        yield
        uy0 = {(d, p): _dot_nt(ar[d, p], s_scr[q, d, p].astype(BF16)) for d, p in units}
        yield
        kv_mask = {d: jnp.concatenate([strict_ref[d], incl_ref[d][:, 2 * T:]], axis=0) for d in range(2)}
        xv = {u: _dot((gm[u][:, 2 * T:] * kv_mask[u[0]]).astype(BF16), v2[u]) for u in units}
        x = {u: uy0[u][:T] + xv[u][:T] for u in units}
        yield
        w = {u: gm[u][:T, :2 * T] * lvl_ref[u[0], 0] for u in units}
        for lb in range(1, 6):
            p_off = {u: gm[u][:T, :2 * T] * lvl_ref[u[0], lb] for u in units}
            t1 = {u: p_off[u] + _dot(p_off[u].astype(BF16), stk(w[u])) for u in units}
            yield
            w = {u: w[u] + t1[u] + _dot(w[u].astype(BF16), stk(t1[u])) for u in units}
            yield
        uu = {u: x[u] + _dot(w[u].astype(BF16), stk(x[u])) for u in units}
        yield
        for d, p in units:
            y_ref = yf_ref if d == 0 else yb_ref
            rb = (gm[d, p][T:, :2 * T] * incl_ref[d][:, :2 * T]).astype(BF16)
            y_ref[q, :, sls[p]] = uy0[d, p][T:] + xv[d, p][T:] + _dot(rb, stk(uu[d, p]))
        yield
        for d, p in units:
            uv = jnp.concatenate([uu[d, p], vv[d][:, sls[p]]], axis=0).astype(BF16)
            s_scr[q, d, p] = (s_scr[q, d, p] * c_tot[d][:, sls[p]]
                              + same_ref[...] * _dot_tn(uv, bkc[d, p]))

    scans = [scan(q) for q in range(WKV_NS)]
    live = [True] * WKV_NS
    tick = 0
    while any(live):
        for q in range(WKV_NS):
            if live[q] and tick >= q * prep_pieces:
                live[q] = next(scans[q], "done") != "done"
        tick += 1

    @pl.when(j == nc - 1)
    def _():
        for q, d, p in units:
            sf_ref[q, d, :, sls[p]] = s_scr[q, d, p, :HEAD_DIM, :] + s_scr[q, d, p, HEAD_DIM:, :]


def _wkv_call(r, k, v, lw, a, kkw, ka, s0, i):
    blk = (WKV_NS, WKV_T, D_MODEL)
    tok = [pl.BlockSpec((None,) + blk, functools.partial(lambda d, b: (_wkv_entry(b, d), 0, 0, 0), d))
           for d in range(2)]
    tok_d = [pl.BlockSpec((None, None) + blk, functools.partial(lambda d, b: (d, _wkv_entry(b, d), 0, 0, 0), d))
             for d in range(2)]
    state = pl.BlockSpec((WKV_NS, 2, HEAD_DIM, D_MODEL), lambda b: (_wkv_decode(b)[0], 0, 0, 0))
    vec = _layer_spec((1, D_MODEL), i)
    masks = _wkv_masks()
    y_shape = jax.ShapeDtypeStruct((N_GROUPED,) + blk, F32)
    return pl.pallas_call(
        _wkv_kernel,
        grid=(WKV_STEPS,),
        in_specs=[tok[0], tok[0], tok[0], tok_d[0], tok_d[0], tok[1], tok[1], tok[1], tok_d[1], tok_d[1],
                  vec, vec] + [_const_spec(m.shape) for m in masks] + [state],
        out_specs=[tok[0], tok[1], state],
        out_shape=[y_shape, y_shape, jax.ShapeDtypeStruct((N_SEQ, 2, HEAD_DIM, D_MODEL), F32)],
        scratch_shapes=[pltpu.VMEM((WKV_NS, 2, N_PAIRS, 2 * HEAD_DIM, LANES), F32)],
        compiler_params=_params(),
        name="wkv_scan",
    )(r, k, v, lw, a, r, k, v, lw, a, kkw, ka, *masks, s0)


def _post_kernel(x_ref, mod_ref, yf_ref, yb_ref, bv_ref, g_ref, lng_ref, lnb_ref, e_ref, et_ref, wo_ref,
                 o_ref):
    rows = (TM_RWKV, D_MODEL)
    gate = mod_ref[...][:, 2 * D_MODEL:3 * D_MODEL]
    y = (yf_ref[...] + yb_ref[...]).reshape(rows)
    e = e_ref[...]
    et = et_ref[...]
    inv_n = 1.0 / HEAD_DIM
    mu = _dot_exact_rhs_stacked(_dot_exact_rhs(y, e), et) * inv_n
    dv = y - mu
    var = _dot_exact_rhs_stacked(_dot_exact_rhs(dv * dv, e), et) * inv_n
    yn = dv * lax.rsqrt(var + GN_EPS) * lng_ref[...] + lnb_ref[...]
    yy = (yn + bv_ref[...].reshape(rows)) * g_ref[...].reshape(rows)
    o_ref[...] = x_ref[...] + gate * _dot(yy.astype(BF16), wo_ref[...])


def _post_call(x, mod, l, yf, yb, bv, g, ln_g, ln_b, e, et, w_o):
    i = l // 2
    tm = TM_RWKV
    row = pl.BlockSpec((tm, D_MODEL), lambda b: (b, 0))
    grouped = _grouped_spec()
    vec = _layer_spec((1, D_MODEL), i)
    return pl.pallas_call(
        _post_kernel,
        grid=(N_TOK // tm,),
        in_specs=[row, _mod_spec(tm, l), grouped, grouped, grouped, grouped, vec, vec,
                  _const_spec((D_MODEL, LANES)), _const_spec((2 * LANES, D_MODEL)),
                  _layer_spec((D_MODEL, D_MODEL), i)],
        out_specs=row,
        out_shape=jax.ShapeDtypeStruct((N_TOK, D_MODEL), F32),
        compiler_params=_params(),
        name="rwkv_out",
    )(x, mod, yf, yb, bv, g, ln_g, ln_b, e, et, w_o)


def _final_kernel(x_ref, g_ref, o_ref):
    x = x_ref[...]
    ms = jnp.mean(x * x, axis=-1, keepdims=True)
    o_ref[...] = x * lax.rsqrt(ms + NORM_EPS) * g_ref[...]


def _final_call(x, g, row0, n_rows):
    tm = TM_NORM
    b0 = row0 // tm
    return pl.pallas_call(
        _final_kernel,
        grid=(n_rows // tm,),
        in_specs=[pl.BlockSpec((tm, D_MODEL), lambda i: (i + b0, 0)), _const_spec((1, D_MODEL))],
        out_specs=pl.BlockSpec((tm, D_MODEL), lambda i: (i, 0)),
        out_shape=jax.ShapeDtypeStruct((n_rows, D_MODEL), F32),
        compiler_params=_params(),
        name="final_norm",
    )(x, g)


def _pad_cols(w, n):
    return jnp.pad(w, [(0, 0)] * (w.ndim - 1) + [(0, n - w.shape[-1])])


def _pad_rows(w, n):
    return jnp.pad(w, [(0, 0)] * (w.ndim - 2) + [(0, n - w.shape[-2]), (0, 0)])


def kernel(x_prompt, x_sample, state_wkv, c, c_ctx, norm1_g, norm2_g, ada_w, ada_b, sgu_w_in, sgu_ln_g,
           sgu_ln_b, sgu_w_s, sgu_b_s, sgu_w_out, rwkv_mu, rwkv_w_r, rwkv_w_k, rwkv_w_v, rwkv_w_o,
           rwkv_w0, rwkv_w1, rwkv_w2, rwkv_a0, rwkv_a1, rwkv_a2, rwkv_v0, rwkv_v1, rwkv_v2, rwkv_g1,
           rwkv_g2, rwkv_k_k, rwkv_k_a, rwkv_r_k, rwkv_ln_g, rwkv_ln_b, mlp_w1, mlp_w2, final_g):
    D = D_MODEL
    bf = lambda w: w.astype(BF16)
    vec = lambda w: w.reshape(w.shape[0], 1, D)

    cond = jnp.concatenate([c_ctx[None, :], c, jnp.zeros((COND_ROWS - 1 - DEC_BATCH, D), F32)], axis=0)
    mod = _ada_call(cond, ada_w, ada_b).reshape(DEPTH, COND_ROWS, 1, 6 * D)

    head_of = jnp.arange(D) // HEAD_DIM
    e = (head_of[:, None] == jnp.arange(LANES)[None, :]).astype(BF16)
    W = dict(
        mu=rwkv_mu, w_r=bf(rwkv_w_r), w_k=bf(rwkv_w_k), w_v=bf(rwkv_w_v),
        g1=bf(rwkv_g1), g2=bf(rwkv_g2),
        w0=rwkv_w0,
        w1=bf(jnp.concatenate([rwkv_w1[:, 0], rwkv_w1[:, 1]], axis=-1)),
        w2=bf(jnp.concatenate([rwkv_w2[:, 0], rwkv_w2[:, 1]], axis=-2)),
        a0=rwkv_a0,
        a1=bf(jnp.concatenate([rwkv_a1[:, 0], rwkv_a1[:, 1]], axis=-1)),
        a2=bf(jnp.concatenate([rwkv_a2[:, 0], rwkv_a2[:, 1]], axis=-2)),
        v0=vec(rwkv_v0), v1=bf(_pad_cols(rwkv_v1, LANES)), v2=bf(_pad_rows(rwkv_v2, LANES)),
        k_a=vec(rwkv_k_a), r_k=rwkv_r_k.reshape(N_RWKV, 1, D), e=e, et=jnp.concatenate([e.T, e.T], axis=0),
    )
    n1 = vec(norm1_g)
    n2 = vec(norm2_g)
    w_pair = bf(sgu_w_s.reshape(-1, N_PAIRS, 2, CHUNK, CHUNK).transpose(0, 1, 3, 2, 4)
                .reshape(-1, N_PAIRS, CHUNK, 2 * CHUNK))
    bmap = jnp.repeat(jnp.swapaxes(sgu_b_s, 1, 2), D // SGU_GROUPS, axis=2)
    sgu_w_in_b, sgu_w_out_b = bf(sgu_w_in), bf(sgu_w_out)
    mlp_w1_b, mlp_w2_b = bf(mlp_w1), bf(mlp_w2)
    w_o_b = bf(rwkv_w_o)
    k_k, ln_g, ln_b = vec(rwkv_k_k), vec(rwkv_ln_g), vec(rwkv_ln_b)
    sgu_g, sgu_b = vec(sgu_ln_g), vec(sgu_ln_b)

    s_lat = state_wkv.transpose(1, 0, 2, 4, 3, 5).reshape(N_RWKV, DEC_BATCH, 2, HEAD_DIM, D)
    s_init = jnp.concatenate([jnp.zeros((N_RWKV, BATCH, 2, HEAD_DIM, D), F32), s_lat], axis=1)

    xs = (x_prompt.reshape(N_CTX, D), x_sample.reshape(N_LAT, D))
    v_first = None
    finals = []
    for l in range(DEPTH):
        i = l // 2
        if l % 2 == 0:
            x = _sgu_call(xs, mod, l, n1, sgu_w_in_b, sgu_g, sgu_b, w_pair, bmap, sgu_w_out_b)
        else:
            r, k, v, g, bv, lw, a = _proj_call(x, mod, l, n1, W, v_first)
            if v_first is None:
                v_first = v
            yf, yb, s_fin = _wkv_call(r, k, v, lw, a, k_k, W['k_a'], s_init[i], i)
            finals.append(s_fin[:BATCH])
            x = _post_call(x, mod, l, yf, yb, bv, g, ln_g, ln_b, W['e'], W['et'], w_o_b)
        x = _mlp_call(x, mod, l, n2, mlp_w1_b, mlp_w2_b)
        xs = (x,)

    y_prompt = _final_call(x, final_g.reshape(1, D), 0, N_CTX).reshape(BATCH, SEQ, D)
    y_sample = _final_call(x, final_g.reshape(1, D), N_CTX, N_LAT).reshape(DEC_BATCH, DEC_SEQ, D)
    new_state = jnp.stack(finals, axis=1).reshape(BATCH, N_RWKV, 2, HEAD_DIM, N_HEADS, HEAD_DIM)
    new_state = new_state.transpose(0, 1, 2, 4, 3, 5)
    return (y_prompt, y_sample, new_state)
```

```python
import functools

import numpy as np
import jax
import jax.numpy as jnp
from jax import lax
from jax.experimental import pallas as pl
from jax.experimental.pallas import tpu as pltpu

D_MODEL = 1024
BATCH = 16
SEQ = 256
DEPTH = 4
DEC_BATCH = 8
DEC_SEQ = 2048
N_RWKV = DEPTH // 2
CHUNK = 128
SGU_GROUPS = 16
HEAD_DIM = 64
N_HEADS = D_MODEL // HEAD_DIM
LORA_V = 32
D_FF = 4 * D_MODEL
NORM_EPS = 1e-6
GN_EPS = 64e-5

LANES = 128
N_PAIRS = D_MODEL // LANES
N_CTX = BATCH * SEQ
N_LAT = DEC_BATCH * DEC_SEQ
N_TOK = N_CTX + N_LAT
N_SEQ = BATCH + DEC_BATCH
COND_ROWS = 16

WKV_T = 64
WKV_NS = 4
NC_CTX = SEQ // WKV_T
NC_LAT = DEC_SEQ // WKV_T
WKV_STEPS_CTX = (BATCH // WKV_NS) * NC_CTX
WKV_STEPS = WKV_STEPS_CTX + (DEC_BATCH // WKV_NS) * NC_LAT

N_GROUPED = N_TOK // (WKV_NS * WKV_T)
TM_RWKV = 256
CHUNKS_PER_BLOCK = TM_RWKV // WKV_T

TM_SGU = 512
TM_MLP = 512
TM_NORM = 512
FF_CHUNK = 1024
ADA_TN = 1536
VMEM_LIMIT = 56 * 1024 * 1024

F32 = jnp.float32
BF16 = jnp.bfloat16


def _dot(a, b):
    return jnp.dot(a, b, preferred_element_type=F32)


def _dot_nt(a, b):
    return lax.dot_general(a, b, (((1,), (1,)), ((), ())), preferred_element_type=F32)


def _dot_tn(a, b):
    return lax.dot_general(a, b, (((0,), (0,)), ((), ())), preferred_element_type=F32)


def _split2(x):
    h1 = x.astype(BF16)
    return h1, (x - h1.astype(F32)).astype(BF16)


def _dot_exact_rhs(x, m):
    h1, h2 = _split2(x)
    return _dot(h1, m) + _dot(h2, m)


def _dot_exact_rhs_stacked(x, m2):
    return _dot(jnp.concatenate(_split2(x), axis=1), m2)


def _dot_exact_lhs(m, x):
    h1, h2 = _split2(x)
    return _dot(m, h1) + _dot(m, h2)


def _sigmoid(x):
    return 1.0 / (1.0 + jnp.exp(-x))


def _norm_mod(x, g, scale, shift):
    ms = jnp.mean(x * x, axis=-1, keepdims=True)
    return (x * lax.rsqrt(ms + NORM_EPS) * g) * (1.0 + scale) + shift


def _stack_heads(x, lo):
    return jnp.concatenate([jnp.where(lo, x, 0.0), jnp.where(lo, 0.0, x)], axis=0)


def _cond_row(i, tm):
    n_ctx_blocks = N_CTX // tm
    return jnp.where(i < n_ctx_blocks, 0, 1 + (i - n_ctx_blocks) // (DEC_SEQ // tm))


def _const_spec(shape):
    nd = len(shape)
    return pl.BlockSpec(shape, lambda *_: (0,) * nd, pipeline_mode=pl.Buffered(1))


def _layer_spec(shape, l):
    nd = len(shape)
    return pl.BlockSpec((None,) + tuple(shape), lambda *_: (l,) + (0,) * nd, pipeline_mode=pl.Buffered(1))


def _mod_spec(tm, l):
    return pl.BlockSpec((None, None, 1, 6 * D_MODEL), lambda i: (l, _cond_row(i, tm), 0, 0))


def _grouped_index(i):
    n_ctx_blocks = N_CTX // TM_RWKV
    per_seq = DEC_SEQ // TM_RWKV
    in_ctx = i < n_ctx_blocks
    il = jnp.maximum(i - n_ctx_blocks, 0)
    seq_l = il // per_seq
    blk = jnp.where(in_ctx, i // WKV_NS, n_ctx_blocks // WKV_NS + (seq_l // WKV_NS) * per_seq + il % per_seq)
    slot = jnp.where(in_ctx, i % WKV_NS, seq_l % WKV_NS)
    return blk, slot


def _grouped_spec():
    return pl.BlockSpec((CHUNKS_PER_BLOCK, None, WKV_T, D_MODEL), lambda i: (*_grouped_index(i), 0, 0))


def _grouped_spec2():
    return pl.BlockSpec((2, CHUNKS_PER_BLOCK, None, WKV_T, D_MODEL), lambda i: (0, *_grouped_index(i), 0, 0))


def _params(n_axes=1):
    return pltpu.CompilerParams(
        dimension_semantics=("arbitrary",) * n_axes, vmem_limit_bytes=VMEM_LIMIT)


def _ada_kernel(cond_ref, w_ref, b_ref, o_ref):
    c = cond_ref[...]
    s = c * _sigmoid(c)
    o_ref[...] = _dot(s.astype(BF16), w_ref[...].astype(BF16)) + b_ref[...]


def _ada_call(cond, ada_w, ada_b):
    return pl.pallas_call(
        _ada_kernel,
        grid=(DEPTH, 6 * D_MODEL // ADA_TN),
        in_specs=[
            pl.BlockSpec((COND_ROWS, D_MODEL), lambda l, n: (0, 0)),
            pl.BlockSpec((None, D_MODEL, ADA_TN), lambda l, n: (l, 0, n)),
            pl.BlockSpec((None, 1, ADA_TN), lambda l, n: (l, 0, n)),
        ],
        out_specs=pl.BlockSpec((None, COND_ROWS, ADA_TN), lambda l, n: (l, 0, n)),
        out_shape=jax.ShapeDtypeStruct((DEPTH, COND_ROWS, 6 * D_MODEL), F32),
        compiler_params=_params(2),
        name="ada_mod",
    )(cond, ada_w, ada_b.reshape(DEPTH, 1, 6 * D_MODEL))


def _sgu_kernel(split_input, *refs):
    if split_input:
        xa_ref, xb_ref, mod_ref, ng_ref, win_ref, lng_ref, lnb_ref, wp_ref, bmap_ref, wout_ref, o_ref = refs
        x = jnp.where(pl.program_id(0) < N_CTX // TM_SGU, xa_ref[...], xb_ref[...])
    else:
        x_ref, mod_ref, ng_ref, win_ref, lng_ref, lnb_ref, wp_ref, bmap_ref, wout_ref, o_ref = refs
        x = x_ref[...]
    mod = mod_ref[...]
    shift, scale, gate = (mod[:, j * D_MODEL:(j + 1) * D_MODEL] for j in range(3))
    h = _norm_mod(x, ng_ref[...], scale, shift)
    z = _dot(h.astype(BF16), win_ref[...])
    z = z * (0.5 * (1.0 + jnp.tanh(0.7978845608028654 * (z + 0.044715 * (z * z * z)))))
    u = z[:, :D_MODEL]
    v = z[:, D_MODEL:]
    mu = jnp.mean(v, axis=-1, keepdims=True)
    dv = v - mu
    var = jnp.mean(dv * dv, axis=-1, keepdims=True)
    v = dv * lax.rsqrt(var + NORM_EPS) * lng_ref[...] + lnb_ref[...]
    lo = lax.broadcasted_iota(jnp.int32, (CHUNK, LANES), 1) < HEAD_DIM
    bmap = bmap_ref[...]
    n_chunks = TM_SGU // CHUNK
    cols = []
    for p in range(N_PAIRS):
        rhs = jnp.concatenate(
            [_stack_heads(v[c * CHUNK:(c + 1) * CHUNK, p * LANES:(p + 1) * LANES], lo) for c in range(n_chunks)],
            axis=1).astype(BF16)
        mixed = _dot(wp_ref[p], rhs)
        cols.append(jnp.concatenate([mixed[:, c * LANES:(c + 1) * LANES] for c in range(n_chunks)], axis=0))
    vs = jnp.concatenate(cols, axis=1) + jnp.concatenate([bmap] * n_chunks, axis=0)
    m = _dot((u * vs).astype(BF16), wout_ref[...])
    o_ref[...] = x + gate * m


def _sgu_call(xs, mod, l, ng, w_in, ln_g, ln_b, w_pair, bmap, w_out):
    i = l // 2
    tm = TM_SGU
    row = pl.BlockSpec((tm, D_MODEL), lambda b: (b, 0))
    if len(xs) == 2:
        n_ctx_blocks = N_CTX // tm
        x_specs = [pl.BlockSpec((tm, D_MODEL), lambda b: (jnp.minimum(b, n_ctx_blocks - 1), 0)),
                   pl.BlockSpec((tm, D_MODEL), lambda b: (jnp.maximum(b - n_ctx_blocks, 0), 0))]
    else:
        x_specs = [row]
    return pl.pallas_call(
        functools.partial(_sgu_kernel, len(xs) == 2),
        grid=(N_TOK // tm,),
        in_specs=x_specs + [
            _mod_spec(tm, l),
            _layer_spec((1, D_MODEL), l),
            _layer_spec((D_MODEL, 2 * D_MODEL), i),
            _layer_spec((1, D_MODEL), i), _layer_spec((1, D_MODEL), i),
            _layer_spec((N_PAIRS, CHUNK, 2 * CHUNK), i),
            _layer_spec((CHUNK, D_MODEL), i),
            _layer_spec((D_MODEL, D_MODEL), i),
        ],
        out_specs=row,
        out_shape=jax.ShapeDtypeStruct((N_TOK, D_MODEL), F32),
        compiler_params=_params(),
        name="sgu_layer",
    )(*xs, mod, ng, w_in, ln_g, ln_b, w_pair, bmap, w_out)


def _mlp_kernel(x_ref, mod_ref, ng_ref, w1_ref, w2_ref, o_ref):
    x = x_ref[...]
    mod = mod_ref[...]
    shift, scale, gate = (mod[:, j * D_MODEL:(j + 1) * D_MODEL] for j in range(3, 6))
    h = _norm_mod(x, ng_ref[...], scale, shift).astype(BF16)
    acc = jnp.zeros((TM_MLP, D_MODEL), F32)
    for c in range(D_FF // FF_CHUNK):
        a = _dot(h, w1_ref[:, c * FF_CHUNK:(c + 1) * FF_CHUNK])
        a = jnp.square(jnp.maximum(a, 0.0))
        acc = acc + _dot(a.astype(BF16), w2_ref[c * FF_CHUNK:(c + 1) * FF_CHUNK, :])
    o_ref[...] = x + gate * acc


def _mlp_call(x, mod, l, ng, w1, w2):
    row = pl.BlockSpec((TM_MLP, D_MODEL), lambda i: (i, 0))
    return pl.pallas_call(
        _mlp_kernel,
        grid=(N_TOK // TM_MLP,),
        in_specs=[row, _mod_spec(TM_MLP, l), _layer_spec((1, D_MODEL), l),
                  _layer_spec((D_MODEL, D_FF), l), _layer_spec((D_FF, D_MODEL), l)],
        out_specs=row,
        out_shape=jax.ShapeDtypeStruct((N_TOK, D_MODEL), F32),
        compiler_params=_params(),
        name="mlp_layer",
    )(x, mod, ng, w1, w2)


def _proj_kernel(has_vmix, *refs):
    if has_vmix:
        (x_ref, xp_ref, xn_ref, mod_ref, ng_ref, mu_ref, wr_ref, wk_ref, wv_ref, g1_ref, g2_ref,
         w0_ref, w1_ref, w2_ref, a0_ref, a1_ref, a2_ref, ka_ref, rk_ref, e_ref, et_ref,
         vf_ref, v0_ref, v1_ref, v2_ref,
         r_out, k_out, v_out, g_out, bv_out, lw_out, a_out) = refs
    else:
        (x_ref, xp_ref, xn_ref, mod_ref, ng_ref, mu_ref, wr_ref, wk_ref, wv_ref, g1_ref, g2_ref,
         w0_ref, w1_ref, w2_ref, a0_ref, a1_ref, a2_ref, ka_ref, rk_ref, e_ref, et_ref,
         r_out, k_out, v_out, g_out, bv_out, lw_out, a_out) = refs
    tm = TM_RWKV
    grouped = (CHUNKS_PER_BLOCK, WKV_T, D_MODEL)
    i = pl.program_id(0)
    t0 = i * tm
    in_ctx = t0 < N_CTX
    pos = jnp.where(in_ctx, t0 & (SEQ - 1), (t0 - N_CTX) & (DEC_SEQ - 1))
    seq_len = jnp.where(in_ctx, SEQ, DEC_SEQ)
    has_prev = jnp.where(pos != 0, 1.0, 0.0)
    has_next = jnp.where(pos + tm != seq_len, 1.0, 0.0)

    mod = mod_ref[...]
    shift, scale = mod[:, :D_MODEL], mod[:, D_MODEL:2 * D_MODEL]
    ng = ng_ref[...]
    h = _norm_mod(x_ref[...], ng, scale, shift)
    h_prev = _norm_mod(xp_ref[7:8, :], ng, scale, shift) * has_prev
    h_next = _norm_mod(xn_ref[0:1, :], ng, scale, shift) * has_next
    rows = lax.broadcasted_iota(jnp.int32, (tm, D_MODEL), 0)
    h_dn = jnp.where(rows == 0, h_prev, pltpu.roll(h, 1, 0))
    h_up = jnp.where(rows == tm - 1, h_next, pltpu.roll(h, tm - 1, 0))
    xx = 0.5 * (h_dn + h_up) - h
    mu = mu_ref[...]
    xr, xw, xk, xv, xa, xg = ((h + xx * mu[j:j + 1, :]).astype(BF16) for j in range(6))

    r = _dot(xr, wr_ref[...])
    k = _dot(xk, wk_ref[...])
    v = _dot(xv, wv_ref[...])
    if has_vmix:
        mix = _sigmoid(v0_ref[...] + _dot(_dot(xv, v1_ref[...]).astype(BF16), v2_ref[...]))
        v = v + (vf_ref[...].reshape(tm, D_MODEL) - v) * mix
    g = _dot(_sigmoid(_dot(xg, g1_ref[...])).astype(BF16), g2_ref[...])

    lo = lax.broadcasted_iota(jnp.int32, (tm, LANES), 1) < HEAD_DIM
    th = jnp.tanh(_dot(xw, w1_ref[...]))
    al = _dot(xa, a1_ref[...])
    a_sum = None
    for d in range(2):
        keep = lo if d == 0 else jnp.logical_not(lo)
        z = w0_ref[d:d + 1, :] + _dot(jnp.where(keep, th, 0.0).astype(BF16), w2_ref[...])
        lw_out[d] = (-0.6065306597126334 * _sigmoid(z)).reshape(grouped)
        a_d = _sigmoid(a0_ref[d:d + 1, :] + _dot(jnp.where(keep, al, 0.0).astype(BF16), a2_ref[...]))
        a_out[d] = a_d.reshape(grouped)
        a_sum = a_d if a_sum is None else a_sum + a_d
    ksum = k * (2.0 + (a_sum - 2.0) * ka_ref[...])
    head_sum = _dot_exact_rhs(r * ksum * rk_ref[...], e_ref[...])
    bonus = _dot_exact_rhs_stacked(head_sum, et_ref[...])
    r_out[...] = r.reshape(grouped)
    k_out[...] = k.reshape(grouped)
    v_out[...] = v.reshape(grouped)
    g_out[...] = g.reshape(grouped)
    bv_out[...] = (bonus * v).reshape(grouped)


def _proj_call(x, mod, l, ng, W, v_first):
    i = l // 2
    tm = TM_RWKV
    nb8 = tm // 8
    has_vmix = v_first is not None
    row = pl.BlockSpec((tm, D_MODEL), lambda b: (b, 0))
    prev = pl.BlockSpec((8, D_MODEL), lambda b: (jnp.maximum(b * nb8 - 1, 0), 0))
    nxt = pl.BlockSpec((8, D_MODEL), lambda b: (jnp.minimum((b + 1) * nb8, N_TOK // 8 - 1), 0))
    mat = _layer_spec((D_MODEL, D_MODEL), i)
    vec = _layer_spec((1, D_MODEL), i)
    vec2 = _layer_spec((2, D_MODEL), i)
    down = _layer_spec((D_MODEL, LANES), i)
    up = _layer_spec((LANES, D_MODEL), i)
    in_specs = [row, prev, nxt, _mod_spec(tm, l), _layer_spec((1, D_MODEL), l), _layer_spec((6, D_MODEL), i),
                mat, mat, mat, down, up,
                vec2, down, up, vec2, down, up, vec, vec,
                _const_spec((D_MODEL, LANES)), _const_spec((2 * LANES, D_MODEL))]
    args = [x, x, x, mod, ng, W['mu'], W['w_r'], W['w_k'], W['w_v'], W['g1'], W['g2'],
            W['w0'], W['w1'], W['w2'], W['a0'], W['a1'], W['a2'], W['k_a'], W['r_k'], W['e'], W['et']]
    if has_vmix:
        in_specs += [_grouped_spec(), _layer_spec((1, D_MODEL), i - 1), _layer_spec((D_MODEL, LANES), i - 1),
                     _layer_spec((LANES, D_MODEL), i - 1)]
        args += [v_first, W['v0'], W['v1'], W['v2']]
    tok = jax.ShapeDtypeStruct((N_GROUPED, WKV_NS, WKV_T, D_MODEL), F32)
    tok2 = jax.ShapeDtypeStruct((2, N_GROUPED, WKV_NS, WKV_T, D_MODEL), F32)
    return pl.pallas_call(
        functools.partial(_proj_kernel, has_vmix),
        grid=(N_TOK // tm,),
        in_specs=in_specs,
        out_specs=[_grouped_spec()] * 5 + [_grouped_spec2()] * 2,
        out_shape=[tok] * 5 + [tok2] * 2,
        compiler_params=_params(),
        name="rwkv_proj",
    )(*args)


def _wkv_decode(i):
    in_ctx = i < WKV_STEPS_CTX
    il = jnp.maximum(i - WKV_STEPS_CTX, 0)
    grp = jnp.where(in_ctx, i // NC_CTX, BATCH // WKV_NS + il // NC_LAT)
    j = jnp.where(in_ctx, i % NC_CTX, il % NC_LAT)
    nc = jnp.where(in_ctx, NC_CTX, NC_LAT)
    base = jnp.where(in_ctx, (i // NC_CTX) * NC_CTX, WKV_STEPS_CTX + (il // NC_LAT) * NC_LAT)
    return grp, j, nc, base


def _wkv_entry(i, d):
    _, j, nc, base = _wkv_decode(i)
    return base + (j if d == 0 else nc - 1 - j)


def _wkv_masks():
    T = WKV_T
    t = np.arange(T)[:, None]
    s1 = np.arange(T)[None, :]
    s2 = np.arange(2 * T)[None, :] % T
    s4 = np.arange(4 * T)[None, :] % T
    cum, strict, incl, lvl = [], [], [], []
    for d in range(2):
        early = (lambda s: s < t) if d == 0 else (lambda s: s > t)
        cum.append(early(s1) | (s1 == t))
        strict.append(early(s2))
        incl.append(early(s4) | (s4 == t))
        lvl.append([early(s2) & (((t ^ s2) >> lb) == 1) for lb in range(6)])
    hd = np.arange(2 * T) // HEAD_DIM
    return (jnp.asarray(np.array(cum), BF16), jnp.asarray(np.array(strict), F32),
            jnp.asarray(np.array(incl), F32), jnp.asarray(np.array(lvl), F32),
            jnp.asarray(hd[:, None] == hd[None, :], F32))


def _wkv_kernel(rf_ref, kf_ref, vf_ref, lwf_ref, af_ref, rb_ref, kb_ref, vb_ref, lwb_ref, ab_ref,
                kkw_ref, ka_ref, cum_ref, strict_ref, incl_ref, lvl_ref, same_ref, s0_ref,
                yf_ref, yb_ref, sf_ref, s_scr):
    T = WKV_T
    _, j, nc, _ = _wkv_decode(pl.program_id(0))
    lo = lax.broadcasted_iota(jnp.int32, (T, LANES), 1) < HEAD_DIM
    pairs = range(N_PAIRS)
    sls = [slice(p * LANES, (p + 1) * LANES) for p in pairs]
    chains = [(q, d) for q in range(WKV_NS) for d in range(2)]
    units = [(q, d, p) for q, d in chains for p in pairs]

    @pl.when(j == 0)
    def _():
        for q, d, p in units:
            s0 = jnp.where(pl.program_id(0) < WKV_STEPS_CTX, 0.0, s0_ref[q, d, :, sls[p]])
            s_scr[q, d, p] = _stack_heads(s0, lo)

    def stk(x):
        return _stack_heads(x.astype(BF16), lo)

    kkw = kkw_ref[...]
    ka = ka_ref[...]
    dir_refs = ((rf_ref, kf_ref, vf_ref, lwf_ref, af_ref), (rb_ref, kb_ref, vb_ref, lwb_ref, ab_ref))
    prep_pieces = 2 * (1 + N_PAIRS // 2)

    def scan(q):
        units = [(d, p) for d in range(2) for p in pairs]
        ar, bk, bkc, v2, vv, c_tot = {}, {}, {}, {}, {}, {}
        for d in range(2):
            r_ref, k_ref, v_ref, lw_ref, a_ref = dir_refs[d]
            lw = lw_ref[q]
            logc = _dot_exact_lhs(cum_ref[d], lw)
            tot = jnp.sum(lw, axis=0, keepdims=True)
            c = jnp.exp(logc)
            c_inv = jnp.exp(-logc)
            c_prev = jnp.exp(logc - lw)
            c_tot[d] = jnp.exp(tot)
            c_end = c_tot[d] * c_inv
            r = r_ref[q]
            k = k_ref[q]
            a = a_ref[q]
            vv[d] = v_ref[q]
            kk_raw = k * kkw
            kd = k * (1.0 + (a - 1.0) * ka)
            yield
            for p in pairs:
                sl = sls[p]
                kq = kk_raw[:, sl]
                sq = kq * kq
                n_lo = jnp.sum(jnp.where(lo, sq, 0.0), axis=-1, keepdims=True)
                n_hi = jnp.sum(jnp.where(lo, 0.0, sq), axis=-1, keepdims=True)
                kk = kq * lax.rsqrt(jnp.maximum(jnp.where(lo, n_lo, n_hi), 1e-24))
                beta = kk * a[:, sl]
                u = (d, p)
                ar[u] = jnp.concatenate([-kk * c_prev[:, sl], r[:, sl] * c[:, sl]], axis=0).astype(BF16)
                bk[u] = jnp.concatenate([stk(beta * c_inv[:, sl]), stk(kd[:, sl] * c_inv[:, sl])], axis=0)
                bkc[u] = jnp.concatenate([beta * c_end[:, sl], kd[:, sl] * c_end[:, sl]], axis=0).astype(BF16)
                v2[u] = stk(vv[d][:, sl])
                if p % 2 == 1:
                    yield
        gm = {u: _dot_nt(ar[u], bk[u]) for u in units}
        yield
        uy0 = {(d, p): _dot_nt(ar[d, p], s_scr[q, d, p].astype(BF16)) for d, p in units}
        yield
        kv_mask = {d: jnp.concatenate([strict_ref[d], incl_ref[d][:, 2 * T:]], axis=0) for d in range(2)}
        xv = {u: _dot((gm[u][:, 2 * T:] * kv_mask[u[0]]).astype(BF16), v2[u]) for u in units}
        x = {u: uy0[u][:T] + xv[u][:T] for u in units}
        yield
        w = {u: gm[u][:T, :2 * T] * lvl_ref[u[0], 0] for u in units}
        for lb in range(1, 6):
            p_off = {u: gm[u][:T, :2 * T] * lvl_ref[u[0], lb] for u in units}
            t1 = {u: p_off[u] + _dot(p_off[u].astype(BF16), stk(w[u])) for u in units}
            yield
            w = {u: w[u] + t1[u] + _dot(w[u].astype(BF16), stk(t1[u])) for u in units}
            yield
        uu = {u: x[u] + _dot(w[u].astype(BF16), stk(x[u])) for u in units}
        yield
        for d, p in units:
            y_ref = yf_ref if d == 0 else yb_ref
            rb = (gm[d, p][T:, :2 * T] * incl_ref[d][:, :2 * T]).astype(BF16)
            y_ref[q, :, sls[p]] = uy0[d, p][T:] + xv[d, p][T:] + _dot(rb, stk(uu[d, p]))
        yield
        for d, p in units:
            uv = jnp.concatenate([uu[d, p], vv[d][:, sls[p]]], axis=0).astype(BF16)
            s_scr[q, d, p] = (s_scr[q, d, p] * c_tot[d][:, sls[p]]
                              + same_ref[...] * _dot_tn(uv, bkc[d, p]))

    scans = [scan(q) for q in range(WKV_NS)]
    live = [True] * WKV_NS
    tick = 0
    while any(live):
        for q in range(WKV_NS):
            if live[q] and tick >= q * prep_pieces:
                live[q] = next(scans[q], "done") != "done"
        tick += 1

    @pl.when(j == nc - 1)
    def _():
        for q, d, p in units:
            sf_ref[q, d, :, sls[p]] = s_scr[q, d, p, :HEAD_DIM, :] + s_scr[q, d, p, HEAD_DIM:, :]


def _wkv_call(r, k, v, lw, a, kkw, ka, s0, i):
    blk = (WKV_NS, WKV_T, D_MODEL)
    tok = [pl.BlockSpec((None,) + blk, functools.partial(lambda d, b: (_wkv_entry(b, d), 0, 0, 0), d))
           for d in range(2)]
    tok_d = [pl.BlockSpec((None, None) + blk, functools.partial(lambda d, b: (d, _wkv_entry(b, d), 0, 0, 0), d))
             for d in range(2)]
    state = pl.BlockSpec((WKV_NS, 2, HEAD_DIM, D_MODEL), lambda b: (_wkv_decode(b)[0], 0, 0, 0))
    cached = pl.BlockSpec((None, WKV_NS, 2, HEAD_DIM, D_MODEL),
                          lambda b: (i, jnp.maximum(_wkv_decode(b)[0] - BATCH // WKV_NS, 0), 0, 0, 0))
    vec = _layer_spec((1, D_MODEL), i)
    masks = _wkv_masks()
    y_shape = jax.ShapeDtypeStruct((N_GROUPED,) + blk, F32)
    return pl.pallas_call(
        _wkv_kernel,
        grid=(WKV_STEPS,),
        in_specs=[tok[0], tok[0], tok[0], tok_d[0], tok_d[0], tok[1], tok[1], tok[1], tok_d[1], tok_d[1],
                  vec, vec] + [_const_spec(m.shape) for m in masks] + [cached],
        out_specs=[tok[0], tok[1], state],
        out_shape=[y_shape, y_shape, jax.ShapeDtypeStruct((N_SEQ, 2, HEAD_DIM, D_MODEL), F32)],
        scratch_shapes=[pltpu.VMEM((WKV_NS, 2, N_PAIRS, 2 * HEAD_DIM, LANES), F32)],
        compiler_params=_params(),
        name="wkv_scan",
    )(r, k, v, lw, a, r, k, v, lw, a, kkw, ka, *masks, s0)


def _post_kernel(x_ref, mod_ref, yf_ref, yb_ref, bv_ref, g_ref, lng_ref, lnb_ref, e_ref, et_ref, wo_ref,
                 o_ref):
    rows = (TM_RWKV, D_MODEL)
    gate = mod_ref[...][:, 2 * D_MODEL:3 * D_MODEL]
    y = (yf_ref[...] + yb_ref[...]).reshape(rows)
    e = e_ref[...]
    et = et_ref[...]
    inv_n = 1.0 / HEAD_DIM
    mu = _dot_exact_rhs_stacked(_dot_exact_rhs(y, e), et) * inv_n
    dv = y - mu
    var = _dot_exact_rhs_stacked(_dot_exact_rhs(dv * dv, e), et) * inv_n
    yn = dv * lax.rsqrt(var + GN_EPS) * lng_ref[...] + lnb_ref[...]
    yy = (yn + bv_ref[...].reshape(rows)) * g_ref[...].reshape(rows)
    o_ref[...] = x_ref[...] + gate * _dot(yy.astype(BF16), wo_ref[...])


def _post_call(x, mod, l, yf, yb, bv, g, ln_g, ln_b, e, et, w_o):
    i = l // 2
    tm = TM_RWKV
    row = pl.BlockSpec((tm, D_MODEL), lambda b: (b, 0))
    grouped = _grouped_spec()
    vec = _layer_spec((1, D_MODEL), i)
    return pl.pallas_call(
        _post_kernel,
        grid=(N_TOK // tm,),
        in_specs=[row, _mod_spec(tm, l), grouped, grouped, grouped, grouped, vec, vec,
                  _const_spec((D_MODEL, LANES)), _const_spec((2 * LANES, D_MODEL)),
                  _layer_spec((D_MODEL, D_MODEL), i)],
        out_specs=row,
        out_shape=jax.ShapeDtypeStruct((N_TOK, D_MODEL), F32),
        compiler_params=_params(),
        name="rwkv_out",
    )(x, mod, yf, yb, bv, g, ln_g, ln_b, e, et, w_o)


def _final_kernel(x_ref, g_ref, o_ref):
    x = x_ref[...]
    ms = jnp.mean(x * x, axis=-1, keepdims=True)
    o_ref[...] = x * lax.rsqrt(ms + NORM_EPS) * g_ref[...]


def _final_call(x, g, row0, n_rows):
    tm = TM_NORM
    b0 = row0 // tm
    return pl.pallas_call(
        _final_kernel,
        grid=(n_rows // tm,),
        in_specs=[pl.BlockSpec((tm, D_MODEL), lambda i: (i + b0, 0)), _const_spec((1, D_MODEL))],
        out_specs=pl.BlockSpec((tm, D_MODEL), lambda i: (i, 0)),
        out_shape=jax.ShapeDtypeStruct((n_rows, D_MODEL), F32),
        compiler_params=_params(),
        name="final_norm",
    )(x, g)


def _pad_cols(w, n):
    return jnp.pad(w, [(0, 0)] * (w.ndim - 1) + [(0, n - w.shape[-1])])


def _pad_rows(w, n):
    return jnp.pad(w, [(0, 0)] * (w.ndim - 2) + [(0, n - w.shape[-2]), (0, 0)])


def kernel(x_prompt, x_sample, state_wkv, c, c_ctx, norm1_g, norm2_g, ada_w, ada_b, sgu_w_in, sgu_ln_g,
           sgu_ln_b, sgu_w_s, sgu_b_s, sgu_w_out, rwkv_mu, rwkv_w_r, rwkv_w_k, rwkv_w_v, rwkv_w_o,
           rwkv_w0, rwkv_w1, rwkv_w2, rwkv_a0, rwkv_a1, rwkv_a2, rwkv_v0, rwkv_v1, rwkv_v2, rwkv_g1,
           rwkv_g2, rwkv_k_k, rwkv_k_a, rwkv_r_k, rwkv_ln_g, rwkv_ln_b, mlp_w1, mlp_w2, final_g):
    D = D_MODEL
    bf = lambda w: w.astype(BF16)
    vec = lambda w: w.reshape(w.shape[0], 1, D)

    cond = jnp.concatenate([c_ctx[None, :], c, jnp.zeros((COND_ROWS - 1 - DEC_BATCH, D), F32)], axis=0)
    mod = _ada_call(cond, ada_w, ada_b).reshape(DEPTH, COND_ROWS, 1, 6 * D)

    head_of = jnp.arange(D) // HEAD_DIM
    e = (head_of[:, None] == jnp.arange(LANES)[None, :]).astype(BF16)
    W = dict(
        mu=rwkv_mu, w_r=bf(rwkv_w_r), w_k=bf(rwkv_w_k), w_v=bf(rwkv_w_v),
        g1=bf(rwkv_g1), g2=bf(rwkv_g2),
        w0=rwkv_w0,
        w1=bf(jnp.concatenate([rwkv_w1[:, 0], rwkv_w1[:, 1]], axis=-1)),
        w2=bf(jnp.concatenate([rwkv_w2[:, 0], rwkv_w2[:, 1]], axis=-2)),
        a0=rwkv_a0,
        a1=bf(jnp.concatenate([rwkv_a1[:, 0], rwkv_a1[:, 1]], axis=-1)),
        a2=bf(jnp.concatenate([rwkv_a2[:, 0], rwkv_a2[:, 1]], axis=-2)),
        v0=vec(rwkv_v0), v1=bf(_pad_cols(rwkv_v1, LANES)), v2=bf(_pad_rows(rwkv_v2, LANES)),
        k_a=vec(rwkv_k_a), r_k=rwkv_r_k.reshape(N_RWKV, 1, D), e=e, et=jnp.concatenate([e.T, e.T], axis=0),
    )
    n1 = vec(norm1_g)
    n2 = vec(norm2_g)
    w_pair = bf(sgu_w_s.reshape(-1, N_PAIRS, 2, CHUNK, CHUNK).transpose(0, 1, 3, 2, 4)
                .reshape(-1, N_PAIRS, CHUNK, 2 * CHUNK))
    bmap = jnp.repeat(jnp.swapaxes(sgu_b_s, 1, 2), D // SGU_GROUPS, axis=2)
    sgu_w_in_b, sgu_w_out_b = bf(sgu_w_in), bf(sgu_w_out)
    mlp_w1_b, mlp_w2_b = bf(mlp_w1), bf(mlp_w2)
    w_o_b = bf(rwkv_w_o)
    k_k, ln_g, ln_b = vec(rwkv_k_k), vec(rwkv_ln_g), vec(rwkv_ln_b)
    sgu_g, sgu_b = vec(sgu_ln_g), vec(sgu_ln_b)

    s_lat = state_wkv.transpose(1, 0, 2, 4, 3, 5).reshape(N_RWKV, DEC_BATCH, 2, HEAD_DIM, D)

    xs = (x_prompt.reshape(N_CTX, D), x_sample.reshape(N_LAT, D))
    v_first = None
    finals = []
    for l in range(DEPTH):
        i = l // 2
        if l % 2 == 0:
            x = _sgu_call(xs, mod, l, n1, sgu_w_in_b, sgu_g, sgu_b, w_pair, bmap, sgu_w_out_b)
        else:
            r, k, v, g, bv, lw, a = _proj_call(x, mod, l, n1, W, v_first)
            if v_first is None:
                v_first = v
            yf, yb, s_fin = _wkv_call(r, k, v, lw, a, k_k, W['k_a'], s_lat, i)
            finals.append(s_fin[:BATCH])
            x = _post_call(x, mod, l, yf, yb, bv, g, ln_g, ln_b, W['e'], W['et'], w_o_b)
        x = _mlp_call(x, mod, l, n2, mlp_w1_b, mlp_w2_b)
        xs = (x,)

    y_prompt = _final_call(x, final_g.reshape(1, D), 0, N_CTX).reshape(BATCH, SEQ, D)
    y_sample = _final_call(x, final_g.reshape(1, D), N_CTX, N_LAT).reshape(DEC_BATCH, DEC_SEQ, D)
    new_state = jnp.stack(finals, axis=1).reshape(BATCH, N_RWKV, 2, HEAD_DIM, N_HEADS, HEAD_DIM)
    new_state = new_state.transpose(0, 1, 2, 4, 3, 5)
    return (y_prompt, y_sample, new_state)
```
